```python
import jax, jax.numpy as jnp
from jax import lax
import numpy as np

D_MODEL = 1024
BATCH = 8
SEQ = 2048
DEPTH = 4

N_A = DEPTH // 2
N_B = DEPTH - N_A
D_RNN = D_MODEL
LRU_BLOCK_W = 256
LRU_BLOCKS = D_RNN // LRU_BLOCK_W
CONV_W = 4
LRU_C = 8.0
N_HEADS = 16
HEAD_DIM = D_MODEL // N_HEADS
Q_BLOCK = 128
D_FF = ((8 * D_MODEL + 3 * 256 - 1) // (3 * 256)) * 256
EPS = 1e-6

kernel_name = "hawk_fox_yoco_hybrid"


def rmsnorm(x, g):
    xf = x.astype(jnp.float32)
    y = xf * lax.rsqrt(jnp.mean(xf * xf, axis=-1, keepdims=True) + EPS)
    return (y * g.astype(jnp.float32)).astype(x.dtype)


def swiglu(x, w_in, w_out):
    gate, up = jnp.split(x @ w_in, 2, axis=-1)
    return (jax.nn.silu(gate) * up) @ w_out


def causal_depthwise_conv(x, w, b):
    S = x.shape[1]
    xp = jnp.pad(x, ((0, 0), (CONV_W - 1, 0), (0, 0)))
    out = b
    for tap in range(CONV_W):
        out = out + xp[:, tap:tap + S] * w[tap]
    return out


def _lin_rec_combine(left, right):
    a1, b1 = left
    a2, b2 = right
    return a1 * a2, a2 * b1 + b2


def rg_lru(x, w_gates, b_gates, lru_param):
    B, S, _ = x.shape
    xb = x.reshape(B, S, LRU_BLOCKS, LRU_BLOCK_W)
    g = (jnp.einsum('bsnw,nwv->bsnv', xb, w_gates) + b_gates).astype(jnp.float32)
    gate_i = jax.nn.sigmoid(g[..., :LRU_BLOCK_W])
    gate_r = jax.nn.sigmoid(g[..., LRU_BLOCK_W:])
    log_a = -LRU_C * gate_r * jax.nn.softplus(-lru_param.astype(jnp.float32).reshape(LRU_BLOCKS, LRU_BLOCK_W))
    a = jnp.exp(log_a)
    mult = jnp.sqrt(-jnp.expm1(2.0 * log_a))
    u = xb.astype(jnp.float32) * gate_i * mult
    a = a.reshape(B, S, D_RNN)
    u = u.reshape(B, S, D_RNN)
    _, h = lax.associative_scan(_lin_rec_combine, (a, u), axis=1)
    return h.astype(x.dtype)


def recurrent_mixer(xn, w_in, conv_w, conv_b, w_gates, b_gates, lru_param, w_out):
    proj = xn @ w_in
    gate_branch, rec = proj[..., :D_RNN], proj[..., D_RNN:]
    rec = causal_depthwise_conv(rec, conv_w, conv_b)
    h = rg_lru(rec, w_gates, b_gates, lru_param)
    return (jax.nn.gelu(gate_branch) * h) @ w_out


def shared_kv_forget(xs, norm_kv, w_kvf, b_forget):
    B, S, _ = xs.shape
    p = rmsnorm(xs, norm_kv) @ w_kvf
    k = p[..., :D_MODEL].reshape(B, S, N_HEADS, HEAD_DIM).transpose(0, 2, 1, 3)
    v = p[..., D_MODEL:2 * D_MODEL].reshape(B, S, N_HEADS, HEAD_DIM).transpose(0, 2, 1, 3)
    f_logit = (p[..., 2 * D_MODEL:] + b_forget).astype(jnp.float32)
    c = jnp.cumsum(jax.nn.log_sigmoid(f_logit), axis=1).transpose(0, 2, 1)
    return k, v, c


def forgetting_attention(xn, w_q, w_o, k, v, c):
    B, S, _ = xn.shape
    q = (xn @ w_q).reshape(B, S, N_HEADS, HEAD_DIM).transpose(0, 2, 1, 3)
    scale = HEAD_DIM ** -0.5
    outs = []
    for blk in range(S // Q_BLOCK):
        q0 = blk * Q_BLOCK
        q1 = q0 + Q_BLOCK
        s = jnp.einsum('bhqd,bhkd->bhqk', q[:, :, q0:q1], k[:, :, :q1],
                       preferred_element_type=jnp.float32) * scale
        s = s + c[:, :, q0:q1, None] - c[:, :, None, :q1]
        mask = (q0 + jnp.arange(Q_BLOCK))[:, None] >= jnp.arange(q1)[None, :]
        s = jnp.where(mask, s, -jnp.inf)
        p = jax.nn.softmax(s, axis=-1)
        outs.append(jnp.einsum('bhqk,bhkd->bhqd', p.astype(v.dtype), v[:, :, :q1]))
    o = jnp.concatenate(outs, axis=2).transpose(0, 2, 1, 3).reshape(B, S, D_MODEL)
    return o @ w_o


def setup_inputs(seed: int = 0) -> dict:
    key = jax.random.key(seed)
    ks = jax.random.split(key, 20)
    f32 = jnp.float32
    nrm = lambda k, shape, fan_in: jax.random.normal(k, shape, f32) * (fan_in ** -0.5)
    gain = lambda k, shape: 1.0 + 0.01 * jax.random.normal(k, shape, f32)

    x = jax.random.normal(ks[0], (BATCH, SEQ, D_MODEL), f32)
    norm_mix = gain(ks[1], (DEPTH, D_MODEL))
    norm_ffn = gain(ks[2], (DEPTH, D_MODEL))
    w_ffn_in = nrm(ks[3], (DEPTH, D_MODEL, 2 * D_FF), D_MODEL)
    w_ffn_out = nrm(ks[4], (DEPTH, D_FF, D_MODEL), D_FF)

    w_rec_in = nrm(ks[5], (N_A, D_MODEL, 2 * D_RNN), D_MODEL)
    conv_w = nrm(ks[6], (N_A, CONV_W, D_RNN), CONV_W)
    conv_b = 0.01 * jax.random.normal(ks[7], (N_A, D_RNN), f32)
    w_lru_gates = nrm(ks[8], (N_A, LRU_BLOCKS, LRU_BLOCK_W, 2 * LRU_BLOCK_W), LRU_BLOCK_W)
    b_lru_gates = 0.01 * jax.random.normal(ks[9], (N_A, LRU_BLOCKS, 2 * LRU_BLOCK_W), f32)
    u = jax.random.uniform(ks[10], (N_A, D_RNN), f32, 0.9, 0.999)
    s = u ** (1.0 / LRU_C)
    lru_param = jnp.log(s) - jnp.log1p(-s)
    w_rec_out = nrm(ks[11], (N_A, D_RNN, D_MODEL), D_RNN)

    norm_kv = gain(ks[12], (D_MODEL,))
    w_kvf = nrm(ks[13], (D_MODEL, 2 * D_MODEL + N_HEADS), D_MODEL)
    b_forget = jnp.linspace(1.0, 5.0, N_HEADS, dtype=f32) + 0.1 * jax.random.normal(ks[14], (N_HEADS,), f32)
    w_q = nrm(ks[15], (N_B, D_MODEL, D_MODEL), D_MODEL)
    w_o = nrm(ks[16], (N_B, D_MODEL, D_MODEL), D_MODEL)
    norm_final = gain(ks[17], (D_MODEL,))
    return {"x": x, "norm_mix": norm_mix, "norm_ffn": norm_ffn, "w_ffn_in": w_ffn_in,
            "w_ffn_out": w_ffn_out, "w_rec_in": w_rec_in, "conv_w": conv_w, "conv_b": conv_b,
            "w_lru_gates": w_lru_gates, "b_lru_gates": b_lru_gates, "lru_param": lru_param,
            "w_rec_out": w_rec_out, "norm_kv": norm_kv, "w_kvf": w_kvf, "b_forget": b_forget,
            "w_q": w_q, "w_o": w_o, "norm_final": norm_final}


def reference(x, norm_mix, norm_ffn, w_ffn_in, w_ffn_out, w_rec_in, conv_w, conv_b,
              w_lru_gates, b_lru_gates, lru_param, w_rec_out, norm_kv, w_kvf, b_forget,
              w_q, w_o, norm_final):
    h = x
    k = v = c = None
    for layer in range(DEPTH):
        xn = rmsnorm(h, norm_mix[layer])
        if layer < N_A:
            i = layer
            mix = recurrent_mixer(xn, w_rec_in[i], conv_w[i], conv_b[i], w_lru_gates[i],
                                  b_lru_gates[i], lru_param[i], w_rec_out[i])
        else:
            if layer == N_A:
                k, v, c = shared_kv_forget(h, norm_kv, w_kvf, b_forget)
            j = layer - N_A
            mix = forgetting_attention(xn, w_q[j], w_o[j], k, v, c)
        h = h + mix
        h = h + swiglu(rmsnorm(h, norm_ffn[layer]), w_ffn_in[layer], w_ffn_out[layer])
    return rmsnorm(h, norm_final)
```

```python
import functools
import math

import jax
import jax.numpy as jnp
from jax import lax
from jax.experimental import pallas as pl
from jax.experimental.pallas import tpu as pltpu

D_MODEL = 1024
DEPTH = 4
N_A = DEPTH // 2
D_RNN = D_MODEL
LRU_BLOCK_W = 256
LRU_BLOCKS = D_RNN // LRU_BLOCK_W
CONV_W = 4
LRU_C = 8.0
N_HEADS = 16
HEAD_DIM = D_MODEL // N_HEADS
D_FF = 2816
EPS = 1e-6

SUBLANES = 8
LANES = 128
VMEM_LIMIT_BYTES = 56 * 1024 * 1024

F32 = jnp.float32
BF16 = jnp.bfloat16

ROW_TILE = 512
SEQ_TILE = 512
ATT_TILE = 256
FFN_CHUNKS = ((0, 512), (512, 512), (1024, 512), (1536, 512), (2048, 512), (2560, 256))


def _rmsnorm(x, g):
    y = x * lax.rsqrt(jnp.mean(x * x, axis=-1, keepdims=True) + EPS)
    return y * g


def _sigmoid(x):
    return 1.0 / (1.0 + jnp.exp(-x))


def _softplus(x):
    return jnp.maximum(x, 0.0) + jnp.log1p(jnp.exp(-jnp.abs(x)))


def _gelu_tanh(x):
    c = math.sqrt(2.0 / math.pi)
    return x * (0.5 * (1.0 + jnp.tanh(c * (x + 0.044715 * (x * x * x)))))


def _dot(a, b):
    return jnp.dot(a, b, preferred_element_type=F32)


def _const_spec(shape):
    n = len(shape)
    return pl.BlockSpec(shape, lambda *_: (0,) * n, pipeline_mode=pl.Buffered(1))


def _params(*sem):
    return pltpu.CompilerParams(dimension_semantics=sem, vmem_limit_bytes=VMEM_LIMIT_BYTES)


def _ffn_kernel(*refs, has_attn, has_final):
    it = iter(refs)
    h_ref = next(it)
    if has_attn:
        o_ref = next(it)
        wo_ref = next(it)
    g_ref = next(it)
    win_ref = next(it)
    wout_ref = next(it)
    if has_final:
        gf_ref = next(it)
    out_ref = next(it)

    x = h_ref[...]
    if has_attn:
        x = x + _dot(o_ref[...], wo_ref[...])
    xn = _rmsnorm(x, g_ref[...]).astype(BF16)
    acc = x
    for c0, cw in FFN_CHUNKS:
        gate = _dot(xn, win_ref[:, c0:c0 + cw])
        up = _dot(xn, win_ref[:, D_FF + c0:D_FF + c0 + cw])
        act = (gate * _sigmoid(gate)) * up
        acc = acc + _dot(act.astype(BF16), wout_ref[c0:c0 + cw, :])
    if has_final:
        acc = _rmsnorm(acc, gf_ref[...])
    out_ref[...] = acc


def _ffn(h2d, g, w_in, w_out, attn=None, final_g=None):
    m = h2d.shape[0]
    row = pl.BlockSpec((ROW_TILE, D_MODEL), lambda i: (i, 0))
    args = [h2d]
    specs = [row]
    if attn is not None:
        o2d, w_o = attn
        args += [o2d, w_o]
        specs += [row, _const_spec((D_MODEL, D_MODEL))]
    args += [g, w_in, w_out]
    specs += [_const_spec((1, D_MODEL)), _const_spec((D_MODEL, 2 * D_FF)), _const_spec((D_FF, D_MODEL))]
    if final_g is not None:
        args.append(final_g)
        specs.append(_const_spec((1, D_MODEL)))
    return pl.pallas_call(
        functools.partial(_ffn_kernel, has_attn=attn is not None, has_final=final_g is not None),
        grid=(m // ROW_TILE,),
        in_specs=specs,
        out_specs=row,
        out_shape=jax.ShapeDtypeStruct((m, D_MODEL), F32),
        compiler_params=_params("parallel"),
        name="ffn",
    )(*args)


def _rec_kernel(h_ref, g_ref, win_ref, cw_ref, cb_ref, wg_ref, bg_ref, lam_ref, wout_ref,
                out_ref, rec_scr, a_scr, u_scr, carry_scr):
    ts = SEQ_TILE

    @pl.when(pl.program_id(1) == 0)
    def _():
        rec_scr[0:SUBLANES, :] = jnp.zeros((SUBLANES, D_RNN), F32)
        carry_scr[...] = jnp.zeros((SUBLANES, D_RNN), F32)

    x = h_ref[0]
    xn = _rmsnorm(x, g_ref[...]).astype(BF16)
    proj = _dot(xn, win_ref[...])
    rec_scr[SUBLANES:SUBLANES + ts, :] = proj[:, D_RNN:]

    conv = cb_ref[...]
    for tap in range(CONV_W):
        off = SUBLANES - (CONV_W - 1 - tap)
        conv = conv + rec_scr[off:off + ts, :] * cw_ref[tap:tap + 1, :]
    rec_scr[0:SUBLANES, :] = rec_scr[ts:ts + SUBLANES, :]

    for n in range(LRU_BLOCKS):
        lo = n * LRU_BLOCK_W
        xb = conv[:, lo:lo + LRU_BLOCK_W]
        gates = _dot(xb.astype(BF16), wg_ref[n]) + bg_ref[n]
        gate_i = _sigmoid(gates[:, :LRU_BLOCK_W])
        gate_r = _sigmoid(gates[:, LRU_BLOCK_W:])
        log_a = (-LRU_C * gate_r) * _softplus(-lam_ref[:, lo:lo + LRU_BLOCK_W])
        a = jnp.exp(log_a)
        a_scr[:, lo:lo + LRU_BLOCK_W] = a
        mult = jnp.sqrt(1.0 - a * a)
        u_scr[:, lo:lo + LRU_BLOCK_W] = xb * gate_i * mult

    row = lax.broadcasted_iota(jnp.int32, (SUBLANES, D_RNN), 0)

    def group(gi, hc):
        r0 = pl.multiple_of(gi * SUBLANES, SUBLANES)
        a = a_scr[pl.ds(r0, SUBLANES), :]
        u = u_scr[pl.ds(r0, SUBLANES), :]
        for k in (1, 2, 4):
            keep = row >= k
            a_sh = jnp.where(keep, pltpu.roll(a, k, 0), 1.0)
            u_sh = jnp.where(keep, pltpu.roll(u, k, 0), 0.0)
            u = a * u_sh + u
            a = a * a_sh
        hs = a * hc + u
        u_scr[pl.ds(r0, SUBLANES), :] = hs
        return jnp.broadcast_to(hs[SUBLANES - 1:SUBLANES, :], (SUBLANES, D_RNN))

    carry_scr[...] = lax.fori_loop(0, ts // SUBLANES, group, carry_scr[...])

    y = _gelu_tanh(proj[:, :D_RNN]) * u_scr[...]
    out_ref[0] = x + _dot(y.astype(BF16), wout_ref[...])


def _rec_layer(h, g, w_in, conv_w, conv_b, w_gates, b_gates, lam, w_out):
    b, s, _ = h.shape
    ts = SEQ_TILE
    tile = pl.BlockSpec((1, ts, D_MODEL), lambda bi, si: (bi, si, 0))
    return pl.pallas_call(
        _rec_kernel,
        grid=(b, s // ts),
        in_specs=[
            tile,
            _const_spec((1, D_MODEL)),
            _const_spec((D_MODEL, 2 * D_RNN)),
            _const_spec((CONV_W, D_RNN)),
            _const_spec((1, D_RNN)),
            _const_spec((LRU_BLOCKS, LRU_BLOCK_W, 2 * LRU_BLOCK_W)),
            _const_spec((LRU_BLOCKS, 1, 2 * LRU_BLOCK_W)),
            _const_spec((1, D_RNN)),
            _const_spec((D_RNN, D_MODEL)),
        ],
        out_specs=tile,
        out_shape=jax.ShapeDtypeStruct(h.shape, F32),
        scratch_shapes=[
            pltpu.VMEM((ts + SUBLANES, D_RNN), F32),
            pltpu.VMEM((ts, D_RNN), F32),
            pltpu.VMEM((ts, D_RNN), F32),
            pltpu.VMEM((SUBLANES, D_RNN), F32),
        ],
        compiler_params=_params("parallel", "arbitrary"),
        name="rec_layer",
    )(h, g, w_in, conv_w, conv_b, w_gates, b_gates, lam, w_out)


def _kvf_kernel(h_ref, g_ref, wkv_ref, wf_ref, bf_ref, k_ref, v_ref, ccol_ref, crow_ref, carry_scr):
    ts = SEQ_TILE

    @pl.when(pl.program_id(1) == 0)
    def _():
        carry_scr[...] = jnp.zeros((1, LANES), F32)

    xn = _rmsnorm(h_ref[0], g_ref[...]).astype(BF16)
    kv = _dot(xn, wkv_ref[...])
    k_ref[0] = kv[:, :D_MODEL].astype(BF16)
    v_ref[0] = kv[:, D_MODEL:].astype(BF16)

    f_logit = _dot(xn, wf_ref[...]) + bf_ref[...]
    c = -_softplus(-f_logit)
    row = lax.broadcasted_iota(jnp.int32, (ts, LANES), 0)
    k = 1
    while k < ts:
        c = c + jnp.where(row >= k, pltpu.roll(c, k, 0), 0.0)
        k *= 2
    c = c + carry_scr[...]
    carry_scr[...] = c[ts - 1:ts, :]
    ccol_ref[0] = c
    crow_ref[0] = c.T[:N_HEADS, :]


def _kvf(h, g, w_kv, w_f, b_f):
    b, s, _ = h.shape
    ts = SEQ_TILE
    tile = pl.BlockSpec((1, ts, D_MODEL), lambda bi, si: (bi, si, 0))
    return pl.pallas_call(
        _kvf_kernel,
        grid=(b, s // ts),
        in_specs=[
            tile,
            _const_spec((1, D_MODEL)),
            _const_spec((D_MODEL, 2 * D_MODEL)),
            _const_spec((D_MODEL, LANES)),
            _const_spec((1, LANES)),
        ],
        out_specs=[
            tile,
            tile,
            pl.BlockSpec((1, ts, LANES), lambda bi, si: (bi, si, 0)),
            pl.BlockSpec((1, N_HEADS, ts), lambda bi, si: (bi, 0, si)),
        ],
        out_shape=[
            jax.ShapeDtypeStruct(h.shape, BF16),
            jax.ShapeDtypeStruct(h.shape, BF16),
            jax.ShapeDtypeStruct((b, s, LANES), F32),
            jax.ShapeDtypeStruct((b, N_HEADS, s), F32),
        ],
        scratch_shapes=[pltpu.VMEM((1, LANES), F32)],
        compiler_params=_params("parallel", "arbitrary"),
        name="kvf",
    )(h, g, w_kv, w_f, b_f)


def _q_kernel(h_ref, g_ref, wq_ref, q_ref):
    xn = _rmsnorm(h_ref[...], g_ref[...]).astype(BF16)
    q_ref[...] = (_dot(xn, wq_ref[...]) * (HEAD_DIM ** -0.5)).astype(BF16)


def _q_proj(h2d, g, w_q):
    m = h2d.shape[0]
    row = pl.BlockSpec((ROW_TILE, D_MODEL), lambda i: (i, 0))
    return pl.pallas_call(
        _q_kernel,
        grid=(m // ROW_TILE,),
        in_specs=[row, _const_spec((1, D_MODEL)), _const_spec((D_MODEL, D_MODEL))],
        out_specs=row,
        out_shape=jax.ShapeDtypeStruct((m, D_MODEL), BF16),
        compiler_params=_params("parallel"),
        name="q_proj",
    )(h2d, g, w_q)


def _attn_kernel(q_ref, k_ref, v_ref, ccol_ref, crow_ref, o_ref):
    t = ATT_TILE
    qi = pl.program_id(1)
    q0 = pl.multiple_of(qi * t, t)
    tri = (lax.broadcasted_iota(jnp.int32, (t, t), 0) >= lax.broadcasted_iota(jnp.int32, (t, t), 1))

    for h in range(N_HEADS):
        lo = h * HEAD_DIM
        q = q_ref[0, :, lo:lo + HEAD_DIM]
        c_t = ccol_ref[0, :, h:h + 1]

        def block(k0, carry, masked):
            m, l, acc = carry
            kb = k_ref[0, pl.ds(k0, t), lo:lo + HEAD_DIM]
            vb = v_ref[0, pl.ds(k0, t), lo:lo + HEAD_DIM]
            c_s = crow_ref[0, h:h + 1, pl.ds(k0, t)]
            s = lax.dot_general(q, kb, (((1,), (1,)), ((), ())), preferred_element_type=F32)
            s = s + c_t - c_s
            if masked:
                s = jnp.where(tri, s, -jnp.inf)
            m_new = jnp.maximum(m, jnp.max(s, axis=-1, keepdims=True))
            alpha = jnp.exp(m - m_new)
            p = jnp.exp(s - m_new)
            l = alpha * l + jnp.sum(p, axis=-1, keepdims=True)
            acc = alpha * acc + _dot(p.astype(BF16), vb)
            return m_new, l, acc

        init = (jnp.full((t, 1), -jnp.inf, F32), jnp.zeros((t, 1), F32), jnp.zeros((t, HEAD_DIM), F32))
        carry = lax.fori_loop(
            0, qi, lambda j, cr: block(pl.multiple_of(j * t, t), cr, False), init)
        _, l, acc = block(q0, carry, True)
        o_ref[0, :, lo:lo + HEAD_DIM] = (acc / l).astype(BF16)


def _attention(q, k, v, ccol, crow):
    b, s, _ = q.shape
    t = ATT_TILE
    return pl.pallas_call(
        _attn_kernel,
        grid=(b, s // t),
        in_specs=[
            pl.BlockSpec((1, t, D_MODEL), lambda bi, qi: (bi, qi, 0)),
            pl.BlockSpec((1, s, D_MODEL), lambda bi, qi: (bi, 0, 0)),
            pl.BlockSpec((1, s, D_MODEL), lambda bi, qi: (bi, 0, 0)),
            pl.BlockSpec((1, t, LANES), lambda bi, qi: (bi, qi, 0)),
            pl.BlockSpec((1, N_HEADS, s), lambda bi, qi: (bi, 0, 0)),
        ],
        out_specs=pl.BlockSpec((1, t, D_MODEL), lambda bi, qi: (bi, qi, 0)),
        out_shape=jax.ShapeDtypeStruct((b, s, D_MODEL), BF16),
        compiler_params=_params("parallel", "arbitrary"),
        name="fox_attention",
    )(q, k, v, ccol, crow)


def kernel(x, norm_mix, norm_ffn, w_ffn_in, w_ffn_out, w_rec_in, conv_w, conv_b, w_lru_gates,
           b_lru_gates, lru_param, w_rec_out, norm_kv, w_kvf, b_forget, w_q, w_o, norm_final):
    b, s, d = x.shape
    m = b * s
    row = lambda a: a.reshape(1, -1)

    h = x
    for i in range(N_A):
        h = _rec_layer(h, row(norm_mix[i]), w_rec_in[i].astype(BF16), conv_w[i], row(conv_b[i]),
                       w_lru_gates[i].astype(BF16), b_lru_gates[i].reshape(LRU_BLOCKS, 1, -1),
                       row(lru_param[i]), w_rec_out[i].astype(BF16))
        h = _ffn(h.reshape(m, d), row(norm_ffn[i]), w_ffn_in[i].astype(BF16),
                 w_ffn_out[i].astype(BF16)).reshape(b, s, d)

    w_kv = w_kvf[:, :2 * D_MODEL].astype(BF16)
    w_f = jnp.pad(w_kvf[:, 2 * D_MODEL:], ((0, 0), (0, LANES - N_HEADS))).astype(BF16)
    b_f = jnp.pad(b_forget, (0, LANES - N_HEADS)).reshape(1, LANES)
    k, v, ccol, crow = _kvf(h, row(norm_kv), w_kv, w_f, b_f)

    h2d = h.reshape(m, d)
    for j in range(DEPTH - N_A):
        layer = N_A + j
        q = _q_proj(h2d, row(norm_mix[layer]), w_q[j].astype(BF16))
        o = _attention(q.reshape(b, s, d), k, v, ccol, crow)
        final_g = row(norm_final) if layer == DEPTH - 1 else None
        h2d = _ffn(h2d, row(norm_ffn[layer]), w_ffn_in[layer].astype(BF16), w_ffn_out[layer].astype(BF16),
                   attn=(o.reshape(m, d), w_o[j].astype(BF16)), final_g=final_g)
    return h2d.reshape(b, s, d)
```

```python
import functools
import math

import jax
import jax.numpy as jnp
from jax import lax
from jax.experimental import pallas as pl
from jax.experimental.pallas import tpu as pltpu

D_MODEL = 1024
DEPTH = 4
N_A = DEPTH // 2
D_RNN = D_MODEL
LRU_BLOCK_W = 256
LRU_BLOCKS = D_RNN // LRU_BLOCK_W
CONV_W = 4
LRU_C = 8.0
N_HEADS = 16
HEAD_DIM = D_MODEL // N_HEADS
D_FF = 2816
EPS = 1e-6

SUBLANES = 8
LANES = 128
VMEM_LIMIT_BYTES = 56 * 1024 * 1024

F32 = jnp.float32
BF16 = jnp.bfloat16

ROW_TILE = 512
SEQ_TILE = 512
ATT_TILE = 256
FFN_CHUNKS = ((0, 512), (512, 512), (1024, 512), (1536, 512), (2048, 512), (2560, 256))


def _rmsnorm(x, g):
    y = x * lax.rsqrt(jnp.mean(x * x, axis=-1, keepdims=True) + EPS)
    return y * g


def _sigmoid(x):
    return 1.0 / (1.0 + jnp.exp(-x))


def _softplus(x):
    return jnp.maximum(x, 0.0) + jnp.log1p(jnp.exp(-jnp.abs(x)))


def _gelu_tanh(x):
    c = math.sqrt(2.0 / math.pi)
    return x * (0.5 * (1.0 + jnp.tanh(c * (x + 0.044715 * (x * x * x)))))


def _dot(a, b):
    return jnp.dot(a, b, preferred_element_type=F32)


def _const_spec(shape):
    n = len(shape)
    return pl.BlockSpec(shape, lambda *_: (0,) * n, pipeline_mode=pl.Buffered(1))


def _params(*sem):
    return pltpu.CompilerParams(dimension_semantics=sem, vmem_limit_bytes=VMEM_LIMIT_BYTES)


def _ffn_kernel(*refs, has_attn, has_final):
    it = iter(refs)
    h_ref = next(it)
    if has_attn:
        o_ref = next(it)
        wo_ref = next(it)
    g_ref = next(it)
    win_ref = next(it)
    wout_ref = next(it)
    if has_final:
        gf_ref = next(it)
    out_ref = next(it)

    x = h_ref[...]
    if has_attn:
        x = x + _dot(o_ref[...], wo_ref[...])
    xn = _rmsnorm(x, g_ref[...]).astype(BF16)
    acc = x
    for c0, cw in FFN_CHUNKS:
        gate = _dot(xn, win_ref[:, c0:c0 + cw])
        up = _dot(xn, win_ref[:, D_FF + c0:D_FF + c0 + cw])
        act = (gate * _sigmoid(gate)) * up
        acc = acc + _dot(act.astype(BF16), wout_ref[c0:c0 + cw, :])
    if has_final:
        acc = _rmsnorm(acc, gf_ref[...])
    out_ref[...] = acc


def _ffn(h2d, g, w_in, w_out, attn=None, final_g=None):
    m = h2d.shape[0]
    row = pl.BlockSpec((ROW_TILE, D_MODEL), lambda i: (i, 0))
    args = [h2d]
    specs = [row]
    if attn is not None:
        o2d, w_o = attn
        args += [o2d, w_o]
        specs += [row, _const_spec((D_MODEL, D_MODEL))]
    args += [g, w_in, w_out]
    specs += [_const_spec((1, D_MODEL)), _const_spec((D_MODEL, 2 * D_FF)), _const_spec((D_FF, D_MODEL))]
    if final_g is not None:
        args.append(final_g)
        specs.append(_const_spec((1, D_MODEL)))
    return pl.pallas_call(
        functools.partial(_ffn_kernel, has_attn=attn is not None, has_final=final_g is not None),
        grid=(m // ROW_TILE,),
        in_specs=specs,
        out_specs=row,
        out_shape=jax.ShapeDtypeStruct((m, D_MODEL), F32),
        compiler_params=_params("parallel"),
        name="ffn",
    )(*args)


def _rec_kernel(h_ref, g_ref, win_ref, cw_ref, cb_ref, wg_ref, bg_ref, lam_ref, wout_ref,
                out_ref, rec_scr, a_scr, u_scr, carry_scr):
    ts = SEQ_TILE

    @pl.when(pl.program_id(1) == 0)
    def _():
        rec_scr[0:SUBLANES, :] = jnp.zeros((SUBLANES, D_RNN), F32)
        carry_scr[...] = jnp.zeros((SUBLANES, D_RNN), F32)

    x = h_ref[0]
    xn = _rmsnorm(x, g_ref[...]).astype(BF16)
    proj = _dot(xn, win_ref[...])
    rec_scr[SUBLANES:SUBLANES + ts, :] = proj[:, D_RNN:]

    conv = cb_ref[...]
    for tap in range(CONV_W):
        off = SUBLANES - (CONV_W - 1 - tap)
        conv = conv + rec_scr[off:off + ts, :] * cw_ref[tap:tap + 1, :]
    rec_scr[0:SUBLANES, :] = rec_scr[ts:ts + SUBLANES, :]

    for n in range(LRU_BLOCKS):
        lo = n * LRU_BLOCK_W
        xb = conv[:, lo:lo + LRU_BLOCK_W]
        gates = _dot(xb.astype(BF16), wg_ref[n]) + bg_ref[n]
        gate_i = _sigmoid(gates[:, :LRU_BLOCK_W])
        gate_r = _sigmoid(gates[:, LRU_BLOCK_W:])
        log_a = (-LRU_C * gate_r) * _softplus(-lam_ref[:, lo:lo + LRU_BLOCK_W])
        a = jnp.exp(log_a)
        a_scr[:, lo:lo + LRU_BLOCK_W] = a
        mult = jnp.sqrt(1.0 - a * a)
        u_scr[:, lo:lo + LRU_BLOCK_W] = xb * gate_i * mult

    row = lax.broadcasted_iota(jnp.int32, (SUBLANES, D_RNN), 0)

    def group(gi, hc):
        r0 = pl.multiple_of(gi * SUBLANES, SUBLANES)
        a = a_scr[pl.ds(r0, SUBLANES), :]
        u = u_scr[pl.ds(r0, SUBLANES), :]
        for k in (1, 2, 4):
            keep = row >= k
            a_sh = jnp.where(keep, pltpu.roll(a, k, 0), 1.0)
            u_sh = jnp.where(keep, pltpu.roll(u, k, 0), 0.0)
            u = a * u_sh + u
            a = a * a_sh
        hs = a * hc + u
        u_scr[pl.ds(r0, SUBLANES), :] = hs
        return jnp.broadcast_to(hs[SUBLANES - 1:SUBLANES, :], (SUBLANES, D_RNN))

    carry_scr[...] = lax.fori_loop(0, ts // SUBLANES, group, carry_scr[...])

    y = _gelu_tanh(proj[:, :D_RNN]) * u_scr[...]
    out_ref[0] = x + _dot(y.astype(BF16), wout_ref[...])


def _rec_layer(h, g, w_in, conv_w, conv_b, w_gates, b_gates, lam, w_out):
    b, s, _ = h.shape
    ts = SEQ_TILE
    tile = pl.BlockSpec((1, ts, D_MODEL), lambda bi, si: (bi, si, 0))
    return pl.pallas_call(
        _rec_kernel,
        grid=(b, s // ts),
        in_specs=[
            tile,
            _const_spec((1, D_MODEL)),
            _const_spec((D_MODEL, 2 * D_RNN)),
            _const_spec((CONV_W, D_RNN)),
            _const_spec((1, D_RNN)),
            _const_spec((LRU_BLOCKS, LRU_BLOCK_W, 2 * LRU_BLOCK_W)),
            _const_spec((LRU_BLOCKS, 1, 2 * LRU_BLOCK_W)),
            _const_spec((1, D_RNN)),
            _const_spec((D_RNN, D_MODEL)),
        ],
        out_specs=tile,
        out_shape=jax.ShapeDtypeStruct(h.shape, F32),
        scratch_shapes=[
            pltpu.VMEM((ts + SUBLANES, D_RNN), F32),
            pltpu.VMEM((ts, D_RNN), F32),
            pltpu.VMEM((ts, D_RNN), F32),
            pltpu.VMEM((SUBLANES, D_RNN), F32),
        ],
        compiler_params=_params("parallel", "arbitrary"),
        name="rec_layer",
    )(h, g, w_in, conv_w, conv_b, w_gates, b_gates, lam, w_out)


AUG_ONES = HEAD_DIM
AUG_C = HEAD_DIM + 3


def _split3(c):
    hi = c.astype(BF16).astype(F32)
    r = c - hi
    mid = r.astype(BF16).astype(F32)
    lo = (r - mid).astype(BF16).astype(F32)
    return hi, mid, lo


def _head_tile(x, col0, h):
    t = x[:, col0 + LANES * (h // 2):col0 + LANES * (h // 2 + 1)]
    return pltpu.roll(t, HEAD_DIM, 1) if h % 2 else t


def _kvf_kernel(h_ref, g_ref, wkv_ref, wf_ref, bf_ref, kt_ref, va_ref, ccol_ref, carry_scr):
    ts = SEQ_TILE

    @pl.when(pl.program_id(1) == 0)
    def _():
        carry_scr[...] = jnp.zeros((1, LANES), F32)

    xn = _rmsnorm(h_ref[0], g_ref[...]).astype(BF16)
    kv = _dot(xn, wkv_ref[...])

    f_logit = _dot(xn, wf_ref[...]) + bf_ref[...]
    c = -_softplus(-f_logit)
    row = lax.broadcasted_iota(jnp.int32, (ts, LANES), 0)
    k = 1
    while k < ts:
        c = c + jnp.where(row >= k, pltpu.roll(c, k, 0), 0.0)
        k *= 2
    c = c + carry_scr[...]
    carry_scr[...] = c[ts - 1:ts, :]
    ccol_ref[0] = c
    c_t = c.T

    lane = lax.broadcasted_iota(jnp.int32, (ts, LANES), 1)
    ones_col = jnp.where(lane == HEAD_DIM, 1.0, 0.0)
    arow = lax.broadcasted_iota(jnp.int32, (HEAD_DIM, ts), 0)
    for h in range(N_HEADS):
        va_ref[0, h] = jnp.where(lane < HEAD_DIM, _head_tile(kv, D_MODEL, h), ones_col).astype(BF16)
        k_t = _head_tile(kv, 0, h).T
        kt_ref[0, h, 0:HEAD_DIM, :] = k_t[0:HEAD_DIM, :].astype(BF16)
        hi, mid, lo = _split3(c_t[h:h + 1, :])
        aug = jnp.where(arow == 0, -hi, jnp.where(arow == 1, -mid, jnp.where(arow == 2, -lo,
                        jnp.where(arow < 6, 1.0, 0.0))))
        kt_ref[0, h, HEAD_DIM:LANES, :] = aug.astype(BF16)


def _kvf(h, g, w_kv, w_f, b_f):
    b, s, _ = h.shape
    ts = SEQ_TILE
    return pl.pallas_call(
        _kvf_kernel,
        grid=(b, s // ts),
        in_specs=[
            pl.BlockSpec((1, ts, D_MODEL), lambda bi, si: (bi, si, 0)),
            _const_spec((1, D_MODEL)),
            _const_spec((D_MODEL, 2 * D_MODEL)),
            _const_spec((D_MODEL, LANES)),
            _const_spec((1, LANES)),
        ],
        out_specs=[
            pl.BlockSpec((1, N_HEADS, LANES, ts), lambda bi, si: (bi, 0, 0, si)),
            pl.BlockSpec((1, N_HEADS, ts, LANES), lambda bi, si: (bi, 0, si, 0)),
            pl.BlockSpec((1, ts, LANES), lambda bi, si: (bi, si, 0)),
        ],
        out_shape=[
            jax.ShapeDtypeStruct((b, N_HEADS, LANES, s), BF16),
            jax.ShapeDtypeStruct((b, N_HEADS, s, LANES), BF16),
            jax.ShapeDtypeStruct((b, s, LANES), F32),
        ],
        scratch_shapes=[pltpu.VMEM((1, LANES), F32)],
        compiler_params=_params("parallel", "arbitrary"),
        name="kvf",
    )(h, g, w_kv, w_f, b_f)


def _q_kernel(h_ref, g_ref, wq_ref, ccol_ref, qa_ref):
    ts = SEQ_TILE
    xn = _rmsnorm(h_ref[0], g_ref[...]).astype(BF16)
    q = _dot(xn, wq_ref[...]) * (HEAD_DIM ** -0.5)
    c = ccol_ref[0]
    lane = lax.broadcasted_iota(jnp.int32, (ts, LANES), 1)
    for h in range(N_HEADS):
        hi, mid, lo = _split3(c[:, h:h + 1])
        aug = jnp.where(lane < AUG_C, 1.0, jnp.where(lane == AUG_C, hi, jnp.where(lane == AUG_C + 1, mid,
                        jnp.where(lane == AUG_C + 2, lo, 0.0))))
        qa_ref[0, h] = jnp.where(lane < AUG_ONES, _head_tile(q, 0, h), aug).astype(BF16)


def _q_proj(h, g, w_q, ccol):
    b, s, _ = h.shape
    ts = SEQ_TILE
    return pl.pallas_call(
        _q_kernel,
        grid=(b, s // ts),
        in_specs=[
            pl.BlockSpec((1, ts, D_MODEL), lambda bi, si: (bi, si, 0)),
            _const_spec((1, D_MODEL)),
            _const_spec((D_MODEL, D_MODEL)),
            pl.BlockSpec((1, ts, LANES), lambda bi, si: (bi, si, 0)),
        ],
        out_specs=pl.BlockSpec((1, N_HEADS, ts, LANES), lambda bi, si: (bi, 0, si, 0)),
        out_shape=jax.ShapeDtypeStruct((b, N_HEADS, s, LANES), BF16),
        compiler_params=_params("parallel", "parallel"),
        name="q_proj",
    )(h, g, w_q, ccol)


def _attn_kernel(qa_ref, kt_ref, va_ref, o_ref):
    t = ATT_TILE
    s_len = qa_ref.shape[2]
    tri = (lax.broadcasted_iota(jnp.int32, (t, t), 0) >= lax.broadcasted_iota(jnp.int32, (t, t), 1))
    for i in range(s_len // t):
        lo = i * t
        outs = []
        for j in range(2):
            q = qa_ref[0, j, lo:lo + t, :]
            s_diag = jnp.where(tri, _dot(q, kt_ref[0, j, :, lo:lo + t]), -jnp.inf)
            m = jnp.max(s_diag, axis=-1, keepdims=True)
            if i > 0:
                s_past = _dot(q, kt_ref[0, j, :, 0:lo])
                m = jnp.maximum(m, jnp.max(s_past, axis=-1, keepdims=True))
            acc = _dot(jnp.exp(s_diag - m).astype(BF16), va_ref[0, j, lo:lo + t, :])
            if i > 0:
                acc = acc + _dot(jnp.exp(s_past - m).astype(BF16), va_ref[0, j, 0:lo, :])
            outs.append(acc[:, :HEAD_DIM] / acc[:, HEAD_DIM:HEAD_DIM + 1])
        o_ref[0, lo:lo + t, :] = jnp.concatenate(outs, axis=1).astype(BF16)


def _attention(qa, kt, va):
    b, _, s, _ = qa.shape
    return pl.pallas_call(
        _attn_kernel,
        grid=(b, N_HEADS // 2),
        in_specs=[
            pl.BlockSpec((1, 2, s, LANES), lambda bi, hp: (bi, hp, 0, 0)),
            pl.BlockSpec((1, 2, LANES, s), lambda bi, hp: (bi, hp, 0, 0)),
            pl.BlockSpec((1, 2, s, LANES), lambda bi, hp: (bi, hp, 0, 0)),
        ],
        out_specs=pl.BlockSpec((1, s, LANES), lambda bi, hp: (bi, 0, hp)),
        out_shape=jax.ShapeDtypeStruct((b, s, D_MODEL), BF16),
        compiler_params=_params("parallel", "parallel"),
        name="fox_attention",
    )(qa, kt, va)


def kernel(x, norm_mix, norm_ffn, w_ffn_in, w_ffn_out, w_rec_in, conv_w, conv_b, w_lru_gates,
           b_lru_gates, lru_param, w_rec_out, norm_kv, w_kvf, b_forget, w_q, w_o, norm_final):
    b, s, d = x.shape
    m = b * s
    row = lambda a: a.reshape(1, -1)

    h = x
    for i in range(N_A):
        h = _rec_layer(h, row(norm_mix[i]), w_rec_in[i].astype(BF16), conv_w[i], row(conv_b[i]),
                       w_lru_gates[i].astype(BF16), b_lru_gates[i].reshape(LRU_BLOCKS, 1, -1),
                       row(lru_param[i]), w_rec_out[i].astype(BF16))
        h = _ffn(h.reshape(m, d), row(norm_ffn[i]), w_ffn_in[i].astype(BF16),
                 w_ffn_out[i].astype(BF16)).reshape(b, s, d)

    w_kv = w_kvf[:, :2 * D_MODEL].astype(BF16)
    w_f = jnp.pad(w_kvf[:, 2 * D_MODEL:], ((0, 0), (0, LANES - N_HEADS))).astype(BF16)
    b_f = jnp.pad(b_forget, (0, LANES - N_HEADS)).reshape(1, LANES)
    kt, va, ccol = _kvf(h, row(norm_kv), w_kv, w_f, b_f)

    h2d = h.reshape(m, d)
    for j in range(DEPTH - N_A):
        layer = N_A + j
        qa = _q_proj(h2d.reshape(b, s, d), row(norm_mix[layer]), w_q[j].astype(BF16), ccol)
        o = _attention(qa, kt, va)
        final_g = row(norm_final) if layer == DEPTH - 1 else None
        h2d = _ffn(h2d, row(norm_ffn[layer]), w_ffn_in[layer].astype(BF16), w_ffn_out[layer].astype(BF16),
                   attn=(o.reshape(m, d), w_o[j].astype(BF16)), final_g=final_g)
    return h2d.reshape(b, s, d)
```

```python
import functools
import math

import jax
import jax.numpy as jnp
from jax import lax
from jax.experimental import pallas as pl
from jax.experimental.pallas import tpu as pltpu

D_MODEL = 1024
DEPTH = 4
N_A = DEPTH // 2
D_RNN = D_MODEL
LRU_BLOCK_W = 256
LRU_BLOCKS = D_RNN // LRU_BLOCK_W
CONV_W = 4
LRU_C = 8.0
N_HEADS = 16
HEAD_DIM = D_MODEL // N_HEADS
D_FF = 2816
EPS = 1e-6

SUBLANES = 8
LANES = 128
VMEM_LIMIT_BYTES = 56 * 1024 * 1024

F32 = jnp.float32
BF16 = jnp.bfloat16

ROW_TILE = 512
SEQ_TILE = 512
ATT_TILE = 256
FFN_CHUNKS = ((0, 512), (512, 512), (1024, 512), (1536, 512), (2048, 512), (2560, 256))


def _rmsnorm(x, g):
    y = x * lax.rsqrt(jnp.mean(x * x, axis=-1, keepdims=True) + EPS)
    return y * g


def _sigmoid(x):
    return 1.0 / (1.0 + jnp.exp(-x))


def _softplus(x):
    return jnp.maximum(x, 0.0) + jnp.log1p(jnp.exp(-jnp.abs(x)))


def _gelu_tanh(x):
    c = math.sqrt(2.0 / math.pi)
    return x * (0.5 * (1.0 + jnp.tanh(c * (x + 0.044715 * (x * x * x)))))


def _dot(a, b):
    return jnp.dot(a, b, preferred_element_type=F32)


def _const_spec(shape):
    n = len(shape)
    return pl.BlockSpec(shape, lambda *_: (0,) * n, pipeline_mode=pl.Buffered(1))


def _params(*sem):
    return pltpu.CompilerParams(dimension_semantics=sem, vmem_limit_bytes=VMEM_LIMIT_BYTES)


def _ffn_kernel(*refs, has_attn, has_final):
    it = iter(refs)
    h_ref = next(it)
    if has_attn:
        o_ref = next(it)
        wo_ref = next(it)
    g_ref = next(it)
    win_ref = next(it)
    wout_ref = next(it)
    if has_final:
        gf_ref = next(it)
    out_ref = next(it)

    x = h_ref[...]
    if has_attn:
        x = x + _dot(o_ref[...], wo_ref[...])
    xn = _rmsnorm(x, g_ref[...]).astype(BF16)
    acc = x
    for c0, cw in FFN_CHUNKS:
        gate = _dot(xn, win_ref[:, c0:c0 + cw])
        up = _dot(xn, win_ref[:, D_FF + c0:D_FF + c0 + cw])
        act = (gate * _sigmoid(gate)) * up
        acc = acc + _dot(act.astype(BF16), wout_ref[c0:c0 + cw, :])
    if has_final:
        acc = _rmsnorm(acc, gf_ref[...])
    out_ref[...] = acc


def _ffn(h2d, g, w_in, w_out, attn=None, final_g=None):
    m = h2d.shape[0]
    row = pl.BlockSpec((ROW_TILE, D_MODEL), lambda i: (i, 0))
    args = [h2d]
    specs = [row]
    if attn is not None:
        o2d, w_o = attn
        args += [o2d, w_o]
        specs += [row, _const_spec((D_MODEL, D_MODEL))]
    args += [g, w_in, w_out]
    specs += [_const_spec((1, D_MODEL)), _const_spec((D_MODEL, 2 * D_FF)), _const_spec((D_FF, D_MODEL))]
    if final_g is not None:
        args.append(final_g)
        specs.append(_const_spec((1, D_MODEL)))
    return pl.pallas_call(
        functools.partial(_ffn_kernel, has_attn=attn is not None, has_final=final_g is not None),
        grid=(m // ROW_TILE,),
        in_specs=specs,
        out_specs=row,
        out_shape=jax.ShapeDtypeStruct((m, D_MODEL), F32),
        compiler_params=_params("parallel"),
        name="ffn",
    )(*args)


REC_STEPS = 64
REC_ROWS = REC_STEPS * SUBLANES
HALO_ROWS = (CONV_W - 1) * SUBLANES


def _rec_kernel(h_ref, g_ref, win_ref, cw_ref, cb_ref, wg_ref, bg_ref, lam_ref, wout_ref,
                out_ref, x_scr, rec_scr, a_scr, u_scr, carry_scr):
    rows = REC_ROWS

    @pl.when(pl.program_id(0) == 0)
    def _():
        rec_scr[0:HALO_ROWS, :] = jnp.zeros((HALO_ROWS, D_RNN), F32)
        carry_scr[...] = jnp.zeros((SUBLANES, D_RNN), F32)

    for t in range(REC_STEPS):
        x_scr[t * SUBLANES:(t + 1) * SUBLANES, :] = h_ref[:, t, :]
    x = x_scr[...]
    xn = _rmsnorm(x, g_ref[...]).astype(BF16)
    proj = _dot(xn, win_ref[...])
    rec_scr[HALO_ROWS:HALO_ROWS + rows, :] = proj[:, D_RNN:]

    conv = cb_ref[...]
    for tap in range(CONV_W):
        conv = conv + rec_scr[tap * SUBLANES:tap * SUBLANES + rows, :] * cw_ref[tap:tap + 1, :]
    rec_scr[0:HALO_ROWS, :] = rec_scr[rows:rows + HALO_ROWS, :]

    for n in range(LRU_BLOCKS):
        lo = n * LRU_BLOCK_W
        xb = conv[:, lo:lo + LRU_BLOCK_W]
        gates = _dot(xb.astype(BF16), wg_ref[n]) + bg_ref[n]
        gate_i = _sigmoid(gates[:, :LRU_BLOCK_W])
        gate_r = _sigmoid(gates[:, LRU_BLOCK_W:])
        log_a = (-LRU_C * gate_r) * _softplus(-lam_ref[:, lo:lo + LRU_BLOCK_W])
        a = jnp.exp(log_a)
        a_scr[:, lo:lo + LRU_BLOCK_W] = a
        mult = jnp.sqrt(1.0 - a * a)
        u_scr[:, lo:lo + LRU_BLOCK_W] = xb * gate_i * mult

    hs = carry_scr[...]
    for t in range(REC_STEPS):
        r0 = t * SUBLANES
        hs = a_scr[r0:r0 + SUBLANES, :] * hs + u_scr[r0:r0 + SUBLANES, :]
        u_scr[r0:r0 + SUBLANES, :] = hs
    carry_scr[...] = hs

    y = _gelu_tanh(proj[:, :D_RNN]) * u_scr[...]
    x_scr[...] = x + _dot(y.astype(BF16), wout_ref[...])
    for t in range(REC_STEPS):
        out_ref[:, t, :] = x_scr[t * SUBLANES:(t + 1) * SUBLANES, :]


def _rec_layer(h, g, w_in, conv_w, conv_b, w_gates, b_gates, lam, w_out):
    b, s, _ = h.shape
    assert b == SUBLANES, "the (time, batch) row layout needs one batch row per sublane"
    tile = pl.BlockSpec((b, REC_STEPS, D_MODEL), lambda si: (0, si, 0))
    rows = REC_ROWS
    return pl.pallas_call(
        _rec_kernel,
        grid=(s // REC_STEPS,),
        in_specs=[
            tile,
            _const_spec((1, D_MODEL)),
            _const_spec((D_MODEL, 2 * D_RNN)),
            _const_spec((CONV_W, D_RNN)),
            _const_spec((1, D_RNN)),
            _const_spec((LRU_BLOCKS, LRU_BLOCK_W, 2 * LRU_BLOCK_W)),
            _const_spec((LRU_BLOCKS, 1, 2 * LRU_BLOCK_W)),
            _const_spec((1, D_RNN)),
            _const_spec((D_RNN, D_MODEL)),
        ],
        out_specs=tile,
        out_shape=jax.ShapeDtypeStruct(h.shape, F32),
        scratch_shapes=[
            pltpu.VMEM((rows, D_MODEL), F32),
            pltpu.VMEM((rows + HALO_ROWS, D_RNN), F32),
            pltpu.VMEM((rows, D_RNN), F32),
            pltpu.VMEM((rows, D_RNN), F32),
            pltpu.VMEM((SUBLANES, D_RNN), F32),
        ],
        compiler_params=_params("arbitrary"),
        name="rec_layer",
    )(h, g, w_in, conv_w, conv_b, w_gates, b_gates, lam, w_out)


PAIR_W = 2 * LANES
N_PAIRS = N_HEADS // 2
C3_ONE = 3 * N_HEADS


def _split3(c):
    hi = c.astype(BF16).astype(F32)
    r = c - hi
    mid = r.astype(BF16).astype(F32)
    lo = (r - mid).astype(BF16).astype(F32)
    return hi, mid, lo


def _aug_placement():
    e = [[0.0] * D_MODEL for _ in range(LANES)]
    for h in range(N_HEADS):
        base = LANES * (h // 2) + HEAD_DIM * (h % 2)
        for piece in range(3):
            e[C3_ONE][base + piece] = 1.0
            e[piece * N_HEADS + h][base + 3 + piece] = 1.0
    return jnp.array(e, BF16)


def _kvf_kernel(h_ref, g_ref, wkv_ref, wf_ref, bf_ref, kt_ref, va_ref, c3_ref, carry_scr):
    ts = SEQ_TILE

    @pl.when(pl.program_id(1) == 0)
    def _():
        carry_scr[...] = jnp.zeros((1, LANES), F32)

    xn = _rmsnorm(h_ref[0], g_ref[...]).astype(BF16)
    kv = _dot(xn, wkv_ref[...])

    f_logit = _dot(xn, wf_ref[...]) + bf_ref[...]
    c = -_softplus(-f_logit)
    row = lax.broadcasted_iota(jnp.int32, (ts, LANES), 0)
    k = 1
    while k < ts:
        c = c + jnp.where(row >= k, pltpu.roll(c, k, 0), 0.0)
        k *= 2
    c = c + carry_scr[...]
    carry_scr[...] = c[ts - 1:ts, :]

    lane = lax.broadcasted_iota(jnp.int32, (ts, LANES), 1)
    hi, mid, lo = _split3(c)
    c3 = jnp.where(lane < N_HEADS, hi,
                   jnp.where(lane < 2 * N_HEADS, pltpu.roll(mid, N_HEADS, 1),
                             jnp.where(lane < C3_ONE, pltpu.roll(lo, 2 * N_HEADS, 1),
                                       jnp.where(lane == C3_ONE, 1.0, 0.0))))
    c3_ref[0] = c3.astype(BF16)

    c_t = c.T
    ones_tile = jnp.where(lane == 0, 1.0, 0.0).astype(BF16)
    trow = lax.broadcasted_iota(jnp.int32, (LANES, ts), 0)
    for p in range(N_PAIRS):
        va_ref[0, p, :, 0:LANES] = kv[:, D_MODEL + LANES * p:D_MODEL + LANES * (p + 1)].astype(BF16)
        va_ref[0, p, :, LANES:PAIR_W] = ones_tile
        k_t = kv[:, LANES * p:LANES * (p + 1)].T
        for j in range(2):
            h = 2 * p + j
            r = trow - HEAD_DIM * j
            own = (r >= 0) & (r < HEAD_DIM)
            kt_ref[0, h, 0:LANES, :] = jnp.where(own, k_t, 0.0).astype(BF16)
            chi, cmid, clo = _split3(c_t[h:h + 1, :])
            aug = jnp.where(r == 0, -chi, jnp.where(r == 1, -cmid, jnp.where(r == 2, -clo,
                            jnp.where((r >= 3) & (r < 6), 1.0, 0.0))))
            kt_ref[0, h, LANES:PAIR_W, :] = aug.astype(BF16)


def _kvf(h, g, w_kv, w_f, b_f):
    b, s, _ = h.shape
    ts = SEQ_TILE
    return pl.pallas_call(
        _kvf_kernel,
        grid=(b, s // ts),
        in_specs=[
            pl.BlockSpec((1, ts, D_MODEL), lambda bi, si: (bi, si, 0)),
            _const_spec((1, D_MODEL)),
            _const_spec((D_MODEL, 2 * D_MODEL)),
            _const_spec((D_MODEL, LANES)),
            _const_spec((1, LANES)),
        ],
        out_specs=[
            pl.BlockSpec((1, N_HEADS, PAIR_W, ts), lambda bi, si: (bi, 0, 0, si)),
            pl.BlockSpec((1, N_PAIRS, ts, PAIR_W), lambda bi, si: (bi, 0, si, 0)),
            pl.BlockSpec((1, ts, LANES), lambda bi, si: (bi, si, 0)),
        ],
        out_shape=[
            jax.ShapeDtypeStruct((b, N_HEADS, PAIR_W, s), BF16),
            jax.ShapeDtypeStruct((b, N_PAIRS, s, PAIR_W), BF16),
            jax.ShapeDtypeStruct((b, s, LANES), BF16),
        ],
        scratch_shapes=[pltpu.VMEM((1, LANES), F32)],
        compiler_params=_params("parallel", "arbitrary"),
        name="kvf",
    )(h, g, w_kv, w_f, b_f)


def _q_kernel(h_ref, g_ref, wq_ref, c3_ref, e_ref, qa_ref):
    xn = _rmsnorm(h_ref[...], g_ref[...]).astype(BF16)
    q = _dot(xn, wq_ref[...]) * (HEAD_DIM ** -0.5)
    aug = _dot(c3_ref[...], e_ref[...])
    for p in range(N_PAIRS):
        qa_ref[:, PAIR_W * p:PAIR_W * p + LANES] = q[:, LANES * p:LANES * (p + 1)].astype(BF16)
        qa_ref[:, PAIR_W * p + LANES:PAIR_W * (p + 1)] = aug[:, LANES * p:LANES * (p + 1)].astype(BF16)


def _q_proj(h2d, g, w_q, c3, e):
    m = h2d.shape[0]
    return pl.pallas_call(
        _q_kernel,
        grid=(m // ROW_TILE,),
        in_specs=[
            pl.BlockSpec((ROW_TILE, D_MODEL), lambda i: (i, 0)),
            _const_spec((1, D_MODEL)),
            _const_spec((D_MODEL, D_MODEL)),
            pl.BlockSpec((ROW_TILE, LANES), lambda i: (i, 0)),
            _const_spec((LANES, D_MODEL)),
        ],
        out_specs=pl.BlockSpec((ROW_TILE, N_PAIRS * PAIR_W), lambda i: (i, 0)),
        out_shape=jax.ShapeDtypeStruct((m, N_PAIRS * PAIR_W), BF16),
        compiler_params=_params("parallel"),
        name="q_proj",
    )(h2d, g, w_q, c3, e)


def _attn_kernel(qa_ref, kt_ref, va_ref, o_ref):
    t = ATT_TILE
    s_len = qa_ref.shape[1]
    tri = (lax.broadcasted_iota(jnp.int32, (t, t), 0) >= lax.broadcasted_iota(jnp.int32, (t, t), 1))
    lane = lax.broadcasted_iota(jnp.int32, (t, LANES), 1)

    def logits(i, j):
        lo = i * t
        s = _dot(qa_ref[0, lo:lo + t, :], kt_ref[0, j, :, 0:lo + t])
        s_diag = jnp.where(tri, s[:, lo:], -jnp.inf)
        s = jnp.concatenate([s[:, :lo], s_diag], axis=1) if i > 0 else s_diag
        return s, jnp.max(s, axis=-1, keepdims=True)

    def weighted_values(i, s, m):
        acc = _dot(jnp.exp(s - m).astype(BF16), va_ref[0, 0, 0:(i + 1) * t, :])
        return acc[:, :LANES] * (1.0 / acc[:, LANES:LANES + 1])

    units = [(i, j) for i in range(s_len // t) for j in range(2)]
    pending = logits(*units[0])
    even = None
    for n, (i, j) in enumerate(units):
        ahead = logits(*units[n + 1]) if n + 1 < len(units) else None
        out = weighted_values(i, *pending)
        pending = ahead
        if j == 0:
            even = out
        else:
            o_ref[0, i * t:(i + 1) * t, :] = jnp.where(lane < HEAD_DIM, even, out).astype(BF16)


def _attention(qa, kt, va):
    b, s, _ = qa.shape
    return pl.pallas_call(
        _attn_kernel,
        grid=(b, N_PAIRS),
        in_specs=[
            pl.BlockSpec((1, s, PAIR_W), lambda bi, p: (bi, 0, p)),
            pl.BlockSpec((1, 2, PAIR_W, s), lambda bi, p: (bi, p, 0, 0)),
            pl.BlockSpec((1, 1, s, PAIR_W), lambda bi, p: (bi, p, 0, 0)),
        ],
        out_specs=pl.BlockSpec((1, s, LANES), lambda bi, p: (bi, 0, p)),
        out_shape=jax.ShapeDtypeStruct((b, s, D_MODEL), BF16),
        compiler_params=_params("parallel", "parallel"),
        name="fox_attention",
    )(qa, kt, va)


def kernel(x, norm_mix, norm_ffn, w_ffn_in, w_ffn_out, w_rec_in, conv_w, conv_b, w_lru_gates,
           b_lru_gates, lru_param, w_rec_out, norm_kv, w_kvf, b_forget, w_q, w_o, norm_final):
    b, s, d = x.shape
    m = b * s
    row = lambda a: a.reshape(1, -1)

    h = x
    for i in range(N_A):
        h = _rec_layer(h, row(norm_mix[i]), w_rec_in[i].astype(BF16), conv_w[i], row(conv_b[i]),
                       w_lru_gates[i].astype(BF16), b_lru_gates[i].reshape(LRU_BLOCKS, 1, -1),
                       row(lru_param[i]), w_rec_out[i].astype(BF16))
        h = _ffn(h.reshape(m, d), row(norm_ffn[i]), w_ffn_in[i].astype(BF16),
                 w_ffn_out[i].astype(BF16)).reshape(b, s, d)

    w_kv = w_kvf[:, :2 * D_MODEL].astype(BF16)
    w_f = jnp.pad(w_kvf[:, 2 * D_MODEL:], ((0, 0), (0, LANES - N_HEADS))).astype(BF16)
    b_f = jnp.pad(b_forget, (0, LANES - N_HEADS)).reshape(1, LANES)
    kt, va, c3 = _kvf(h, row(norm_kv), w_kv, w_f, b_f)
    c3 = c3.reshape(m, LANES)
    e = _aug_placement()

    h2d = h.reshape(m, d)
    for j in range(DEPTH - N_A):
        layer = N_A + j
        qa = _q_proj(h2d, row(norm_mix[layer]), w_q[j].astype(BF16), c3, e)
        o = _attention(qa.reshape(b, s, N_PAIRS * PAIR_W), kt, va)
        final_g = row(norm_final) if layer == DEPTH - 1 else None
        h2d = _ffn(h2d, row(norm_ffn[layer]), w_ffn_in[layer].astype(BF16), w_ffn_out[layer].astype(BF16),
                   attn=(o.reshape(m, d), w_o[j].astype(BF16)), final_g=final_g)
    return h2d.reshape(b, s, d)
```

```python
import functools
import math

import jax
import jax.numpy as jnp
from jax import lax
from jax.experimental import pallas as pl
from jax.experimental.pallas import tpu as pltpu

D_MODEL = 1024
DEPTH = 4
N_A = DEPTH // 2
D_RNN = D_MODEL
LRU_BLOCK_W = 256
LRU_BLOCKS = D_RNN // LRU_BLOCK_W
CONV_W = 4
LRU_C = 8.0
N_HEADS = 16
HEAD_DIM = D_MODEL // N_HEADS
D_FF = 2816
EPS = 1e-6

SUBLANES = 8
LANES = 128
BF16_ROWS = 2 * SUBLANES
VMEM_LIMIT_BYTES = 56 * 1024 * 1024

F32 = jnp.float32
BF16 = jnp.bfloat16

ROW_TILE = 512
SEQ_TILE = 512
ATT_TILE = 256
FFN_CHUNKS = ((0, 512), (512, 512), (1024, 512), (1536, 512), (2048, 512), (2560, 256))


def _rmsnorm(x, g):
    y = x * lax.rsqrt(jnp.mean(x * x, axis=-1, keepdims=True) + EPS)
    return y * g


def _sigmoid(x):
    return 1.0 / (1.0 + jnp.exp(-x))


def _softplus(x):
    return jnp.maximum(x, 0.0) + jnp.log1p(jnp.exp(-jnp.abs(x)))


def _gelu_tanh(x):
    c = math.sqrt(2.0 / math.pi)
    return x * (0.5 * (1.0 + jnp.tanh(c * (x + 0.044715 * (x * x * x)))))


def _dot(a, b):
    return jnp.dot(a, b, preferred_element_type=F32)


def _const_spec(shape):
    n = len(shape)
    return pl.BlockSpec(shape, lambda *_: (0,) * n, pipeline_mode=pl.Buffered(1))


def _layer_spec(shape, layer):
    n = len(shape)
    return pl.BlockSpec((None,) + tuple(shape), lambda *_: (layer,) + (0,) * n, pipeline_mode=pl.Buffered(1))


def _params(*sem):
    return pltpu.CompilerParams(dimension_semantics=sem, vmem_limit_bytes=VMEM_LIMIT_BYTES)


PAIR_W = 2 * LANES
N_PAIRS = N_HEADS // 2
QA_W = N_PAIRS * PAIR_W
C3_ONE = 3 * N_HEADS
N_PIECES = 3


def _split3(c):
    hi = c.astype(BF16).astype(F32)
    r = c - hi
    mid = r.astype(BF16).astype(F32)
    lo = (r - mid).astype(BF16).astype(F32)
    return hi, mid, lo


def _aug_placement():
    e = [[0.0] * D_MODEL for _ in range(LANES)]
    for h in range(N_HEADS):
        base = LANES * (h // 2) + HEAD_DIM * (h % 2)
        for piece in range(N_PIECES):
            e[C3_ONE][base + piece] = 1.0
            e[piece * N_HEADS + h][base + N_PIECES + piece] = 1.0
    return jnp.array(e, BF16)


def _emit_query(x, g, wq_ref, c3, e_ref, qa_ref):
    xn = _rmsnorm(x, g).astype(BF16)
    q = _dot(xn, wq_ref[...]) * (HEAD_DIM ** -0.5)
    aug = _dot(c3, e_ref[...])
    for p in range(N_PAIRS):
        qa_ref[:, PAIR_W * p:PAIR_W * p + LANES] = q[:, LANES * p:LANES * (p + 1)].astype(BF16)
        qa_ref[:, PAIR_W * p + LANES:PAIR_W * (p + 1)] = aug[:, LANES * p:LANES * (p + 1)].astype(BF16)


def _ffn_kernel(*refs, has_attn, has_final, has_query):
    it = iter(refs)
    h_ref = next(it)
    if has_attn:
        o_ref, wo_ref = next(it), next(it)
    g_ref, win_ref, wout_ref = next(it), next(it), next(it)
    if has_final:
        gf_ref = next(it)
    if has_query:
        gq_ref, wq_ref, c3_ref, e_ref = next(it), next(it), next(it), next(it)
    out_ref = next(it)

    x = h_ref[...]
    if has_attn:
        x = x + _dot(o_ref[...], wo_ref[...])
    xn = _rmsnorm(x, g_ref[...]).astype(BF16)
    acc = x
    for c0, cw in FFN_CHUNKS:
        gate = _dot(xn, win_ref[:, c0:c0 + cw])
        up = _dot(xn, win_ref[:, D_FF + c0:D_FF + c0 + cw])
        act = (gate * _sigmoid(gate)) * up
        acc = acc + _dot(act.astype(BF16), wout_ref[c0:c0 + cw, :])
    if has_final:
        acc = _rmsnorm(acc, gf_ref[...])
    out_ref[...] = acc
    if has_query:
        _emit_query(acc, gq_ref[...], wq_ref, c3_ref[...], e_ref, next(it))


def _ffn(h2d, layer, norm_ffn, w_in, w_out, attn=None, final_g=None, query=None):
    m = h2d.shape[0]
    row = pl.BlockSpec((ROW_TILE, D_MODEL), lambda i: (i, 0))
    args = [h2d]
    specs = [row]
    if attn is not None:
        o2d, w_o, j = attn
        args += [o2d, w_o]
        specs += [row, _layer_spec((D_MODEL, D_MODEL), j)]
    args += [norm_ffn, w_in, w_out]
    specs += [_layer_spec((1, D_MODEL), layer), _layer_spec((D_MODEL, 2 * D_FF), layer),
              _layer_spec((D_FF, D_MODEL), layer)]
    if final_g is not None:
        args.append(final_g)
        specs.append(_const_spec((1, D_MODEL)))
    out_specs = row
    out_shape = jax.ShapeDtypeStruct((m, D_MODEL), F32)
    if query is not None:
        norm_mix, w_q, j, c3, e = query
        args += [norm_mix, w_q, c3, e]
        specs += [_layer_spec((1, D_MODEL), layer + 1), _layer_spec((D_MODEL, D_MODEL), j),
                  pl.BlockSpec((ROW_TILE, LANES), lambda i: (i, 0)), _const_spec((LANES, D_MODEL))]
        out_specs = [row, pl.BlockSpec((ROW_TILE, QA_W), lambda i: (i, 0))]
        out_shape = [out_shape, jax.ShapeDtypeStruct((m, QA_W), BF16)]
    return pl.pallas_call(
        functools.partial(_ffn_kernel, has_attn=attn is not None, has_final=final_g is not None,
                          has_query=query is not None),
        grid=(m // ROW_TILE,),
        in_specs=specs,
        out_specs=out_specs,
        out_shape=out_shape,
        compiler_params=_params("parallel"),
        name="ffn",
    )(*args)


REC_STEPS = 64
REC_ROWS = REC_STEPS * SUBLANES
HALO_ROWS = (CONV_W - 1) * SUBLANES


def _rec_kernel(h_ref, g_ref, win_ref, cw_ref, cb_ref, wg_ref, bg_ref, lam_ref, wout_ref,
                out_ref, x_scr, rec_scr, a_scr, u_scr, carry_scr):
    rows = REC_ROWS

    @pl.when(pl.program_id(0) == 0)
    def _():
        rec_scr[0:HALO_ROWS, :] = jnp.zeros((HALO_ROWS, D_RNN), F32)
        carry_scr[...] = jnp.zeros((SUBLANES, D_RNN), F32)

    for t in range(REC_STEPS):
        x_scr[t * SUBLANES:(t + 1) * SUBLANES, :] = h_ref[:, t, :]
    x = x_scr[...]
    xn = _rmsnorm(x, g_ref[...]).astype(BF16)
    proj = _dot(xn, win_ref[...])
    rec_scr[HALO_ROWS:HALO_ROWS + rows, :] = proj[:, D_RNN:]

    conv = cb_ref[...]
    for tap in range(CONV_W):
        conv = conv + rec_scr[tap * SUBLANES:tap * SUBLANES + rows, :] * cw_ref[tap:tap + 1, :]
    rec_scr[0:HALO_ROWS, :] = rec_scr[rows:rows + HALO_ROWS, :]

    for n in range(LRU_BLOCKS):
        lo = n * LRU_BLOCK_W
        xb = conv[:, lo:lo + LRU_BLOCK_W]
        gates = _dot(xb.astype(BF16), wg_ref[n]) + bg_ref[n]
        gate_i = _sigmoid(gates[:, :LRU_BLOCK_W])
        gate_r = _sigmoid(gates[:, LRU_BLOCK_W:])
        log_a = (-LRU_C * gate_r) * _softplus(-lam_ref[:, lo:lo + LRU_BLOCK_W])
        a = jnp.exp(log_a)
        a_scr[:, lo:lo + LRU_BLOCK_W] = a
        mult = jnp.sqrt(1.0 - a * a)
        u_scr[:, lo:lo + LRU_BLOCK_W] = xb * gate_i * mult

    hs = carry_scr[...]
    for t in range(REC_STEPS):
        r0 = t * SUBLANES
        hs = a_scr[r0:r0 + SUBLANES, :] * hs + u_scr[r0:r0 + SUBLANES, :]
        u_scr[r0:r0 + SUBLANES, :] = hs
    carry_scr[...] = hs

    y = _gelu_tanh(proj[:, :D_RNN]) * u_scr[...]
    x_scr[...] = x + _dot(y.astype(BF16), wout_ref[...])
    for t in range(REC_STEPS):
        out_ref[:, t, :] = x_scr[t * SUBLANES:(t + 1) * SUBLANES, :]


def _rec_layer(h, i, norm_mix, w_in, conv_w, conv_b, w_gates, b_gates, lam, w_out):
    b, s, _ = h.shape
    assert b == SUBLANES, "the (time, batch) row layout needs one batch row per sublane"
    tile = pl.BlockSpec((b, REC_STEPS, D_MODEL), lambda si: (0, si, 0))
    rows = REC_ROWS
    return pl.pallas_call(
        _rec_kernel,
        grid=(s // REC_STEPS,),
        in_specs=[
            tile,
            _layer_spec((1, D_MODEL), i),
            _layer_spec((D_MODEL, 2 * D_RNN), i),
            _layer_spec((CONV_W, D_RNN), i),
            _layer_spec((1, D_RNN), i),
            _layer_spec((LRU_BLOCKS, LRU_BLOCK_W, 2 * LRU_BLOCK_W), i),
            _layer_spec((LRU_BLOCKS, 1, 2 * LRU_BLOCK_W), i),
            _layer_spec((1, D_RNN), i),
            _layer_spec((D_RNN, D_MODEL), i),
        ],
        out_specs=tile,
        out_shape=jax.ShapeDtypeStruct(h.shape, F32),
        scratch_shapes=[
            pltpu.VMEM((rows, D_MODEL), F32),
            pltpu.VMEM((rows + HALO_ROWS, D_RNN), F32),
            pltpu.VMEM((rows, D_RNN), F32),
            pltpu.VMEM((rows, D_RNN), F32),
            pltpu.VMEM((SUBLANES, D_RNN), F32),
        ],
        compiler_params=_params("arbitrary"),
        name="rec_layer",
    )(h, norm_mix, w_in, conv_w, conv_b, w_gates, b_gates, lam, w_out)


def _kvf_kernel(h_ref, g_ref, wkv_ref, wf_ref, bf_ref, gq_ref, wq_ref, e_ref,
                kt_ref, va_ref, c3_ref, qa_ref, carry_scr):
    ts = SEQ_TILE

    @pl.when(pl.program_id(1) == 0)
    def _():
        carry_scr[...] = jnp.zeros((1, LANES), F32)

    x = h_ref[0]
    xn = _rmsnorm(x, g_ref[...]).astype(BF16)
    kv = _dot(xn, wkv_ref[...])

    f_logit = _dot(xn, wf_ref[...]) + bf_ref[...]
    c = -_softplus(-f_logit)
    row = lax.broadcasted_iota(jnp.int32, (ts, LANES), 0)
    k = 1
    while k < ts:
        c = c + jnp.where(row >= k, pltpu.roll(c, k, 0), 0.0)
        k *= 2
    c = c + carry_scr[...]
    carry_scr[...] = c[ts - 1:ts, :]

    lane = lax.broadcasted_iota(jnp.int32, (ts, LANES), 1)
    hi, mid, lo = _split3(c)
    c3 = jnp.where(lane < N_HEADS, hi,
                   jnp.where(lane < 2 * N_HEADS, pltpu.roll(mid, N_HEADS, 1),
                             jnp.where(lane < C3_ONE, pltpu.roll(lo, 2 * N_HEADS, 1),
                                       jnp.where(lane == C3_ONE, 1.0, 0.0)))).astype(BF16)
    c3_ref[0] = c3
    _emit_query(x, gq_ref[...], wq_ref, c3, e_ref, qa_ref.at[0])

    c_t = c.T
    ones_tile = jnp.where(lane == 0, 1.0, 0.0).astype(BF16)
    trow = lax.broadcasted_iota(jnp.int32, (LANES, ts), 0)
    arow = lax.broadcasted_iota(jnp.int32, (BF16_ROWS, ts), 0)
    for p in range(N_PAIRS):
        va_ref[0, p, :, 0:LANES] = kv[:, D_MODEL + LANES * p:D_MODEL + LANES * (p + 1)].astype(BF16)
        va_ref[0, p, :, LANES:PAIR_W] = ones_tile
        k_t = kv[:, LANES * p:LANES * (p + 1)].T
        for j in range(2):
            h = 2 * p + j
            own = (trow >= HEAD_DIM * j) & (trow < HEAD_DIM * (j + 1))
            kt_ref[0, h, 0:LANES, :] = jnp.where(own, k_t, 0.0).astype(BF16)
            chi, cmid, clo = _split3(c_t[h:h + 1, :])
            aug = jnp.where(arow == 0, -chi, jnp.where(arow == 1, -cmid, jnp.where(arow == 2, -clo,
                            jnp.where(arow < 2 * N_PIECES, 1.0, 0.0))))
            kt_ref[0, h, LANES:PAIR_W, :] = jnp.zeros((LANES, ts), BF16)
            a0 = LANES + HEAD_DIM * j
            kt_ref[0, h, a0:a0 + BF16_ROWS, :] = aug.astype(BF16)


def _kvf(h, norm_kv, w_kv, w_f, b_f, norm_mix, w_q, e):
    b, s, _ = h.shape
    ts = SEQ_TILE
    return pl.pallas_call(
        _kvf_kernel,
        grid=(b, s // ts),
        in_specs=[
            pl.BlockSpec((1, ts, D_MODEL), lambda bi, si: (bi, si, 0)),
            _const_spec((1, D_MODEL)),
            _const_spec((D_MODEL, 2 * D_MODEL)),
            _const_spec((D_MODEL, LANES)),
            _const_spec((1, LANES)),
            _layer_spec((1, D_MODEL), N_A),
            _layer_spec((D_MODEL, D_MODEL), 0),
            _const_spec((LANES, D_MODEL)),
        ],
        out_specs=[
            pl.BlockSpec((1, N_HEADS, PAIR_W, ts), lambda bi, si: (bi, 0, 0, si)),
            pl.BlockSpec((1, N_PAIRS, ts, PAIR_W), lambda bi, si: (bi, 0, si, 0)),
            pl.BlockSpec((1, ts, LANES), lambda bi, si: (bi, si, 0)),
            pl.BlockSpec((1, ts, QA_W), lambda bi, si: (bi, si, 0)),
        ],
        out_shape=[
            jax.ShapeDtypeStruct((b, N_HEADS, PAIR_W, s), BF16),
            jax.ShapeDtypeStruct((b, N_PAIRS, s, PAIR_W), BF16),
            jax.ShapeDtypeStruct((b, s, LANES), BF16),
            jax.ShapeDtypeStruct((b, s, QA_W), BF16),
        ],
        scratch_shapes=[pltpu.VMEM((1, LANES), F32)],
        compiler_params=_params("parallel", "arbitrary"),
        name="kvf",
    )(h, norm_kv, w_kv, w_f, b_f, norm_mix, w_q, e)


def _attn_kernel(qa_ref, kt_ref, va_ref, o_ref):
    t = ATT_TILE
    s_len = qa_ref.shape[1]
    tri = (lax.broadcasted_iota(jnp.int32, (t, t), 0) >= lax.broadcasted_iota(jnp.int32, (t, t), 1))
    lane = lax.broadcasted_iota(jnp.int32, (t, LANES), 1)

    def logits(i, j):
        lo = i * t
        s = _dot(qa_ref[0, lo:lo + t, :], kt_ref[0, j, :, 0:lo + t])
        s_diag = jnp.where(tri, s[:, lo:], -jnp.inf)
        s = jnp.concatenate([s[:, :lo], s_diag], axis=1) if i > 0 else s_diag
        return s, jnp.max(s, axis=-1, keepdims=True)

    def weighted_values(i, s, m):
        acc = _dot(jnp.exp(s - m).astype(BF16), va_ref[0, 0, 0:(i + 1) * t, :])
        return acc[:, :LANES] * (1.0 / acc[:, LANES:LANES + 1])

    units = [(i, j) for i in range(s_len // t) for j in range(2)]
    pending = logits(*units[0])
    even = None
    for n, (i, j) in enumerate(units):
        ahead = logits(*units[n + 1]) if n + 1 < len(units) else None
        out = weighted_values(i, *pending)
        pending = ahead
        if j == 0:
            even = out
        else:
            o_ref[0, i * t:(i + 1) * t, :] = jnp.where(lane < HEAD_DIM, even, out).astype(BF16)


def _attention(qa, kt, va):
    b, s, _ = qa.shape
    return pl.pallas_call(
        _attn_kernel,
        grid=(b, N_PAIRS),
        in_specs=[
            pl.BlockSpec((1, s, PAIR_W), lambda bi, p: (bi, 0, p)),
            pl.BlockSpec((1, 2, PAIR_W, s), lambda bi, p: (bi, p, 0, 0)),
            pl.BlockSpec((1, 1, s, PAIR_W), lambda bi, p: (bi, p, 0, 0)),
        ],
        out_specs=pl.BlockSpec((1, s, LANES), lambda bi, p: (bi, 0, p)),
        out_shape=jax.ShapeDtypeStruct((b, s, D_MODEL), BF16),
        compiler_params=_params("parallel", "parallel"),
        name="fox_attention",
    )(qa, kt, va)


def kernel(x, norm_mix, norm_ffn, w_ffn_in, w_ffn_out, w_rec_in, conv_w, conv_b, w_lru_gates,
           b_lru_gates, lru_param, w_rec_out, norm_kv, w_kvf, b_forget, w_q, w_o, norm_final):
    b, s, d = x.shape
    m = b * s
    row = lambda a: a.reshape(1, -1)
    rows = lambda a: a.reshape(a.shape[0], 1, a.shape[1])

    norm_mix, norm_ffn = rows(norm_mix), rows(norm_ffn)
    w_ffn_in, w_ffn_out = w_ffn_in.astype(BF16), w_ffn_out.astype(BF16)
    w_rec_in, w_rec_out = w_rec_in.astype(BF16), w_rec_out.astype(BF16)
    w_lru_gates = w_lru_gates.astype(BF16)
    b_lru_gates = b_lru_gates.reshape(N_A, LRU_BLOCKS, 1, 2 * LRU_BLOCK_W)
    conv_b, lru_param = rows(conv_b), rows(lru_param)
    w_q, w_o = w_q.astype(BF16), w_o.astype(BF16)

    h = x
    for i in range(N_A):
        h = _rec_layer(h, i, norm_mix, w_rec_in, conv_w, conv_b, w_lru_gates, b_lru_gates, lru_param, w_rec_out)
        h = _ffn(h.reshape(m, d), i, norm_ffn, w_ffn_in, w_ffn_out).reshape(b, s, d)

    w_kv = w_kvf[:, :2 * D_MODEL].astype(BF16)
    w_f = jnp.pad(w_kvf[:, 2 * D_MODEL:], ((0, 0), (0, LANES - N_HEADS))).astype(BF16)
    b_f = jnp.pad(b_forget, (0, LANES - N_HEADS)).reshape(1, LANES)
    e = _aug_placement()
    kt, va, c3, qa = _kvf(h, row(norm_kv), w_kv, w_f, b_f, norm_mix, w_q, e)
    c3 = c3.reshape(m, LANES)

    h2d = h.reshape(m, d)
    for j in range(DEPTH - N_A):
        layer = N_A + j
        o = _attention(qa, kt, va)
        attn = (o.reshape(m, d), w_o, j)
        if layer == DEPTH - 1:
            h2d = _ffn(h2d, layer, norm_ffn, w_ffn_in, w_ffn_out, attn=attn, final_g=row(norm_final))
        else:
            h2d, qa = _ffn(h2d, layer, norm_ffn, w_ffn_in, w_ffn_out, attn=attn,
                           query=(norm_mix, w_q, j + 1, c3, e))
            qa = qa.reshape(b, s, QA_W)
    return h2d.reshape(b, s, d)
```

```python
import functools
import math

import jax
import jax.numpy as jnp
from jax import lax
from jax.experimental import pallas as pl
from jax.experimental.pallas import tpu as pltpu

D_MODEL = 1024
DEPTH = 4
N_A = DEPTH // 2
D_RNN = D_MODEL
LRU_BLOCK_W = 256
LRU_BLOCKS = D_RNN // LRU_BLOCK_W
CONV_W = 4
LRU_C = 8.0
N_HEADS = 16
HEAD_DIM = D_MODEL // N_HEADS
D_FF = 2816
EPS = 1e-6

SUBLANES = 8
LANES = 128
BF16_ROWS = 2 * SUBLANES
VMEM_LIMIT_BYTES = 56 * 1024 * 1024
TINY = 1e-30

F32 = jnp.float32
BF16 = jnp.bfloat16

ROW_TILE = 512
FFN_ROW_TILE = 1024
SEQ_TILE = 512
ATT_TILE = 256
FFN_CHUNKS = ((0, 512), (512, 512), (1024, 512), (1536, 512), (2048, 512), (2560, 256))


def _rmsnorm(x, g):
    y = x * lax.rsqrt(jnp.mean(x * x, axis=-1, keepdims=True) + EPS)
    return y * g


def _sigmoid(x):
    return 1.0 / (1.0 + jnp.exp(-x))


def _softplus(x):
    return jnp.maximum(x, 0.0) + jnp.log1p(jnp.exp(-jnp.abs(x)))


def _sigmoid_tanh(x):
    return 0.5 * jnp.tanh(0.5 * x) + 0.5


def _gelu_tanh(x):
    c = math.sqrt(2.0 / math.pi)
    half = 0.5 * x
    return half + half * jnp.tanh(x * (c + (c * 0.044715) * (x * x)))


def _dot(a, b):
    return jnp.dot(a, b, preferred_element_type=F32)


def _const_spec(shape):
    n = len(shape)
    return pl.BlockSpec(shape, lambda *_: (0,) * n, pipeline_mode=pl.Buffered(1))


def _layer_spec(shape, layer):
    n = len(shape)
    return pl.BlockSpec((None,) + tuple(shape), lambda *_: (layer,) + (0,) * n, pipeline_mode=pl.Buffered(1))


def _params(*sem):
    return pltpu.CompilerParams(dimension_semantics=sem, vmem_limit_bytes=VMEM_LIMIT_BYTES)


PAIR_W = 2 * LANES
N_PAIRS = N_HEADS // 2
QA_W = N_PAIRS * PAIR_W
C3_ONE = 3 * N_HEADS
N_PIECES = 3


def _split3(c):
    hi = c.astype(BF16).astype(F32)
    r = c - hi
    mid = r.astype(BF16).astype(F32)
    lo = (r - mid).astype(BF16).astype(F32)
    return hi, mid, lo


def _aug_placement():
    e = [[0.0] * D_MODEL for _ in range(LANES)]
    for h in range(N_HEADS):
        base = LANES * (h // 2) + HEAD_DIM * (h % 2)
        for piece in range(N_PIECES):
            e[C3_ONE][base + piece] = 1.0
            e[piece * N_HEADS + h][base + N_PIECES + piece] = 1.0
    return jnp.array(e, BF16)


def _emit_query(x, g, wq_ref, c3, e_ref, qa_ref):
    xn = _rmsnorm(x, g).astype(BF16)
    q = _dot(xn, wq_ref[...]) * (HEAD_DIM ** -0.5)
    aug = _dot(c3, e_ref[...])
    for p in range(N_PAIRS):
        qa_ref[:, PAIR_W * p:PAIR_W * p + LANES] = q[:, LANES * p:LANES * (p + 1)].astype(BF16)
        qa_ref[:, PAIR_W * p + LANES:PAIR_W * (p + 1)] = aug[:, LANES * p:LANES * (p + 1)].astype(BF16)


def _ffn_kernel(*refs, has_attn, has_final, has_query):
    it = iter(refs)
    h_ref = next(it)
    if has_attn:
        o_ref, wo_ref = next(it), next(it)
    g_ref, win_ref, wout_ref = next(it), next(it), next(it)
    if has_final:
        gf_ref = next(it)
    if has_query:
        gq_ref, wq_ref, c3_ref, e_ref = next(it), next(it), next(it), next(it)
    out_ref = next(it)

    x = h_ref[...]
    if has_attn:
        x = x + _dot(o_ref[...], wo_ref[...])
    xn = _rmsnorm(x, g_ref[...]).astype(BF16)
    acc = x
    for c0, cw in FFN_CHUNKS:
        gate = _dot(xn, win_ref[:, c0:c0 + cw])
        up = _dot(xn, win_ref[:, D_FF + c0:D_FF + c0 + cw])
        act = (gate * _sigmoid(gate)) * up
        acc = acc + _dot(act.astype(BF16), wout_ref[c0:c0 + cw, :])
    if has_final:
        acc = _rmsnorm(acc, gf_ref[...])
    out_ref[...] = acc
    if has_query:
        _emit_query(acc, gq_ref[...], wq_ref, c3_ref[...], e_ref, next(it))


def _ffn(h2d, layer, norm_ffn, w_in, w_out, attn=None, final_g=None, query=None):
    m = h2d.shape[0]
    tile = ROW_TILE if query is not None else FFN_ROW_TILE
    row = pl.BlockSpec((tile, D_MODEL), lambda i: (i, 0))
    args = [h2d]
    specs = [row]
    if attn is not None:
        o2d, w_o, j = attn
        args += [o2d, w_o]
        specs += [row, _layer_spec((D_MODEL, D_MODEL), j)]
    args += [norm_ffn, w_in, w_out]
    specs += [_layer_spec((1, D_MODEL), layer), _layer_spec((D_MODEL, 2 * D_FF), layer),
              _layer_spec((D_FF, D_MODEL), layer)]
    if final_g is not None:
        args.append(final_g)
        specs.append(_const_spec((1, D_MODEL)))
    out_specs = row
    out_shape = jax.ShapeDtypeStruct((m, D_MODEL), F32)
    if query is not None:
        norm_mix, w_q, j, c3, e = query
        args += [norm_mix, w_q, c3, e]
        specs += [_layer_spec((1, D_MODEL), layer + 1), _layer_spec((D_MODEL, D_MODEL), j),
                  pl.BlockSpec((tile, LANES), lambda i: (i, 0)), _const_spec((LANES, D_MODEL))]
        out_specs = [row, pl.BlockSpec((tile, QA_W), lambda i: (i, 0))]
        out_shape = [out_shape, jax.ShapeDtypeStruct((m, QA_W), BF16)]
    return pl.pallas_call(
        functools.partial(_ffn_kernel, has_attn=attn is not None, has_final=final_g is not None,
                          has_query=query is not None),
        grid=(m // tile,),
        in_specs=specs,
        out_specs=out_specs,
        out_shape=out_shape,
        compiler_params=_params("parallel"),
        name="ffn",
    )(*args)


REC_STEPS = 64
REC_ROWS = REC_STEPS * SUBLANES
HALO_ROWS = (CONV_W - 1) * SUBLANES


def _rec_kernel(h_ref, g_ref, win_ref, cw_ref, cb_ref, wg_ref, bg_ref, lam_ref, wout_ref,
                out_ref, x_scr, rec_scr, a_scr, u_scr, carry_scr):
    rows = REC_ROWS

    @pl.when(pl.program_id(0) == 0)
    def _():
        rec_scr[0:HALO_ROWS, :] = jnp.zeros((HALO_ROWS, D_RNN), F32)
        carry_scr[...] = jnp.zeros((SUBLANES, D_RNN), F32)

    for t in range(REC_STEPS):
        x_scr[t * SUBLANES:(t + 1) * SUBLANES, :] = h_ref[:, t, :]
    x = x_scr[...]
    xn = _rmsnorm(x, g_ref[...]).astype(BF16)
    proj = _dot(xn, win_ref[...])
    rec_scr[HALO_ROWS:HALO_ROWS + rows, :] = proj[:, D_RNN:]

    conv = cb_ref[...]
    for tap in range(CONV_W):
        conv = conv + rec_scr[tap * SUBLANES:tap * SUBLANES + rows, :] * cw_ref[tap:tap + 1, :]
    rec_scr[0:HALO_ROWS, :] = rec_scr[rows:rows + HALO_ROWS, :]

    for n in range(LRU_BLOCKS):
        lo = n * LRU_BLOCK_W
        xb = conv[:, lo:lo + LRU_BLOCK_W]
        gates = _dot(xb.astype(BF16), wg_ref[n]) + bg_ref[n]
        gate_i = _sigmoid_tanh(gates[:, :LRU_BLOCK_W])
        gate_r = _sigmoid_tanh(gates[:, LRU_BLOCK_W:])
        log_a = gate_r * (-LRU_C * _softplus(-lam_ref[:, lo:lo + LRU_BLOCK_W]))
        a = jnp.exp(log_a)
        a_scr[:, lo:lo + LRU_BLOCK_W] = a
        one_m_a2 = 1.0 - a * a
        mult = one_m_a2 * lax.rsqrt(jnp.maximum(one_m_a2, TINY))
        u_scr[:, lo:lo + LRU_BLOCK_W] = xb * gate_i * mult

    hs = carry_scr[...]
    for t in range(REC_STEPS):
        r0 = t * SUBLANES
        hs = a_scr[r0:r0 + SUBLANES, :] * hs + u_scr[r0:r0 + SUBLANES, :]
        u_scr[r0:r0 + SUBLANES, :] = hs
    carry_scr[...] = hs
    y = _gelu_tanh(proj[:, :D_RNN]) * u_scr[...]
    x_scr[...] = x + _dot(y.astype(BF16), wout_ref[...])
    for t in range(REC_STEPS):
        out_ref[:, t, :] = x_scr[t * SUBLANES:(t + 1) * SUBLANES, :]


def _rec_layer(h, i, norm_mix, w_in, conv_w, conv_b, w_gates, b_gates, lam, w_out):
    b, s, _ = h.shape
    assert b == SUBLANES, "the (time, batch) row layout needs one batch row per sublane"
    tile = pl.BlockSpec((b, REC_STEPS, D_MODEL), lambda si: (0, si, 0))
    rows = REC_ROWS
    return pl.pallas_call(
        _rec_kernel,
        grid=(s // REC_STEPS,),
        in_specs=[
            tile,
            _layer_spec((1, D_MODEL), i),
            _layer_spec((D_MODEL, 2 * D_RNN), i),
            _layer_spec((CONV_W, D_RNN), i),
            _layer_spec((1, D_RNN), i),
            _layer_spec((LRU_BLOCKS, LRU_BLOCK_W, 2 * LRU_BLOCK_W), i),
            _layer_spec((LRU_BLOCKS, 1, 2 * LRU_BLOCK_W), i),
            _layer_spec((1, D_RNN), i),
            _layer_spec((D_RNN, D_MODEL), i),
        ],
        out_specs=tile,
        out_shape=jax.ShapeDtypeStruct(h.shape, F32),
        scratch_shapes=[
            pltpu.VMEM((rows, D_MODEL), F32),
            pltpu.VMEM((rows + HALO_ROWS, D_RNN), F32),
            pltpu.VMEM((rows, D_RNN), F32),
            pltpu.VMEM((rows, D_RNN), F32),
            pltpu.VMEM((SUBLANES, D_RNN), F32),
        ],
        compiler_params=_params("arbitrary"),
        name="rec_layer",
    )(h, norm_mix, w_in, conv_w, conv_b, w_gates, b_gates, lam, w_out)


def _kvf_kernel(h_ref, g_ref, wkv_ref, wf_ref, bf_ref, gq_ref, wq_ref, e_ref,
                kt_ref, va_ref, c3_ref, qa_ref, carry_scr):
    ts = SEQ_TILE

    @pl.when(pl.program_id(1) == 0)
    def _():
        carry_scr[...] = jnp.zeros((1, LANES), F32)

    x = h_ref[0]
    xn = _rmsnorm(x, g_ref[...]).astype(BF16)
    kv = _dot(xn, wkv_ref[...])

    f_logit = _dot(xn, wf_ref[...]) + bf_ref[...]
    c = -_softplus(-f_logit)
    row = lax.broadcasted_iota(jnp.int32, (ts, LANES), 0)
    k = 1
    while k < ts:
        c = c + jnp.where(row >= k, pltpu.roll(c, k, 0), 0.0)
        k *= 2
    c = c + carry_scr[...]
    carry_scr[...] = c[ts - 1:ts, :]

    lane = lax.broadcasted_iota(jnp.int32, (ts, LANES), 1)
    hi, mid, lo = _split3(c)
    c3 = jnp.where(lane < N_HEADS, hi,
                   jnp.where(lane < 2 * N_HEADS, pltpu.roll(mid, N_HEADS, 1),
                             jnp.where(lane < C3_ONE, pltpu.roll(lo, 2 * N_HEADS, 1),
                                       jnp.where(lane == C3_ONE, 1.0, 0.0)))).astype(BF16)
    c3_ref[0] = c3
    _emit_query(x, gq_ref[...], wq_ref, c3, e_ref, qa_ref.at[0])

    c_t = c.T
    ones_tile = jnp.where(lane == 0, 1.0, 0.0).astype(BF16)
    trow = lax.broadcasted_iota(jnp.int32, (LANES, ts), 0)
    arow = lax.broadcasted_iota(jnp.int32, (BF16_ROWS, ts), 0)
    for p in range(N_PAIRS):
        va_ref[0, p, :, 0:LANES] = kv[:, D_MODEL + LANES * p:D_MODEL + LANES * (p + 1)].astype(BF16)
        va_ref[0, p, :, LANES:PAIR_W] = ones_tile
        k_t = kv[:, LANES * p:LANES * (p + 1)].T
        for j in range(2):
            h = 2 * p + j
            own = (trow >= HEAD_DIM * j) & (trow < HEAD_DIM * (j + 1))
            kt_ref[0, h, 0:LANES, :] = jnp.where(own, k_t, 0.0).astype(BF16)
            chi, cmid, clo = _split3(c_t[h:h + 1, :])
            aug = jnp.where(arow == 0, -chi, jnp.where(arow == 1, -cmid, jnp.where(arow == 2, -clo,
                            jnp.where(arow < 2 * N_PIECES, 1.0, 0.0))))
            kt_ref[0, h, LANES:PAIR_W, :] = jnp.zeros((LANES, ts), BF16)
            a0 = LANES + HEAD_DIM * j
            kt_ref[0, h, a0:a0 + BF16_ROWS, :] = aug.astype(BF16)


def _kvf(h, norm_kv, w_kv, w_f, b_f, norm_mix, w_q, e):
    b, s, _ = h.shape
    ts = SEQ_TILE
    return pl.pallas_call(
        _kvf_kernel,
        grid=(b, s // ts),
        in_specs=[
            pl.BlockSpec((1, ts, D_MODEL), lambda bi, si: (bi, si, 0)),
            _const_spec((1, D_MODEL)),
            _const_spec((D_MODEL, 2 * D_MODEL)),
            _const_spec((D_MODEL, LANES)),
            _const_spec((1, LANES)),
            _layer_spec((1, D_MODEL), N_A),
            _layer_spec((D_MODEL, D_MODEL), 0),
            _const_spec((LANES, D_MODEL)),
        ],
        out_specs=[
            pl.BlockSpec((1, N_HEADS, PAIR_W, ts), lambda bi, si: (bi, 0, 0, si)),
            pl.BlockSpec((1, N_PAIRS, ts, PAIR_W), lambda bi, si: (bi, 0, si, 0)),
            pl.BlockSpec((1, ts, LANES), lambda bi, si: (bi, si, 0)),
            pl.BlockSpec((1, ts, QA_W), lambda bi, si: (bi, si, 0)),
        ],
        out_shape=[
            jax.ShapeDtypeStruct((b, N_HEADS, PAIR_W, s), BF16),
            jax.ShapeDtypeStruct((b, N_PAIRS, s, PAIR_W), BF16),
            jax.ShapeDtypeStruct((b, s, LANES), BF16),
            jax.ShapeDtypeStruct((b, s, QA_W), BF16),
        ],
        scratch_shapes=[pltpu.VMEM((1, LANES), F32)],
        compiler_params=_params("parallel", "arbitrary"),
        name="kvf",
    )(h, norm_kv, w_kv, w_f, b_f, norm_mix, w_q, e)


def _attn_kernel(qa_ref, kt_ref, va_ref, o_ref):
    t = ATT_TILE
    s_len = qa_ref.shape[1]
    tri = (lax.broadcasted_iota(jnp.int32, (t, t), 0) >= lax.broadcasted_iota(jnp.int32, (t, t), 1))
    lane = lax.broadcasted_iota(jnp.int32, (t, LANES), 1)

    def logits(i, j):
        lo = i * t
        s = _dot(qa_ref[0, lo:lo + t, :], kt_ref[0, j, :, 0:lo + t])
        s_diag = jnp.where(tri, s[:, lo:], -jnp.inf)
        s = jnp.concatenate([s[:, :lo], s_diag], axis=1) if i > 0 else s_diag
        return s, jnp.max(s, axis=-1, keepdims=True)

    def weighted_values(i, s, m):
        acc = _dot(jnp.exp(s - m).astype(BF16), va_ref[0, 0, 0:(i + 1) * t, :])
        return acc[:, :LANES] * (1.0 / acc[:, LANES:LANES + 1])

    units = [(i, j) for i in range(s_len // t) for j in range(2)]
    pending = logits(*units[0])
    even = None
    for n, (i, j) in enumerate(units):
        ahead = logits(*units[n + 1]) if n + 1 < len(units) else None
        out = weighted_values(i, *pending)
        pending = ahead
        if j == 0:
            even = out
        else:
            o_ref[0, i * t:(i + 1) * t, :] = jnp.where(lane < HEAD_DIM, even, out).astype(BF16)


def _attention(qa, kt, va):
    b, s, _ = qa.shape
    return pl.pallas_call(
        _attn_kernel,
        grid=(b, N_PAIRS),
        in_specs=[
            pl.BlockSpec((1, s, PAIR_W), lambda bi, p: (bi, 0, p)),
            pl.BlockSpec((1, 2, PAIR_W, s), lambda bi, p: (bi, p, 0, 0)),
            pl.BlockSpec((1, 1, s, PAIR_W), lambda bi, p: (bi, p, 0, 0)),
        ],
        out_specs=pl.BlockSpec((1, s, LANES), lambda bi, p: (bi, 0, p)),
        out_shape=jax.ShapeDtypeStruct((b, s, D_MODEL), BF16),
        compiler_params=_params("parallel", "parallel"),
        name="fox_attention",
    )(qa, kt, va)


def kernel(x, norm_mix, norm_ffn, w_ffn_in, w_ffn_out, w_rec_in, conv_w, conv_b, w_lru_gates,
           b_lru_gates, lru_param, w_rec_out, norm_kv, w_kvf, b_forget, w_q, w_o, norm_final):
    b, s, d = x.shape
    m = b * s
    row = lambda a: a.reshape(1, -1)
    rows = lambda a: a.reshape(a.shape[0], 1, a.shape[1])

    norm_mix, norm_ffn = rows(norm_mix), rows(norm_ffn)
    w_ffn_in, w_ffn_out = w_ffn_in.astype(BF16), w_ffn_out.astype(BF16)
    w_rec_in, w_rec_out = w_rec_in.astype(BF16), w_rec_out.astype(BF16)
    w_lru_gates = w_lru_gates.astype(BF16)
    b_lru_gates = b_lru_gates.reshape(N_A, LRU_BLOCKS, 1, 2 * LRU_BLOCK_W)
    conv_b, lru_param = rows(conv_b), rows(lru_param)
    w_q, w_o = w_q.astype(BF16), w_o.astype(BF16)

    h = x
    for i in range(N_A):
        h = _rec_layer(h, i, norm_mix, w_rec_in, conv_w, conv_b, w_lru_gates, b_lru_gates, lru_param, w_rec_out)
        h = _ffn(h.reshape(m, d), i, norm_ffn, w_ffn_in, w_ffn_out).reshape(b, s, d)

    w_kv = w_kvf[:, :2 * D_MODEL].astype(BF16)
    w_f = jnp.pad(w_kvf[:, 2 * D_MODEL:], ((0, 0), (0, LANES - N_HEADS))).astype(BF16)
    b_f = jnp.pad(b_forget, (0, LANES - N_HEADS)).reshape(1, LANES)
    e = _aug_placement()
    kt, va, c3, qa = _kvf(h, row(norm_kv), w_kv, w_f, b_f, norm_mix, w_q, e)
    c3 = c3.reshape(m, LANES)

    h2d = h.reshape(m, d)
    for j in range(DEPTH - N_A):
        layer = N_A + j
        o = _attention(qa, kt, va)
        attn = (o.reshape(m, d), w_o, j)
        if layer == DEPTH - 1:
            h2d = _ffn(h2d, layer, norm_ffn, w_ffn_in, w_ffn_out, attn=attn, final_g=row(norm_final))
        else:
            h2d, qa = _ffn(h2d, layer, norm_ffn, w_ffn_in, w_ffn_out, attn=attn,
                           query=(norm_mix, w_q, j + 1, c3, e))
            qa = qa.reshape(b, s, QA_W)
    return h2d.reshape(b, s, d)
```

```python
import functools
import math

import jax
import jax.numpy as jnp
from jax import lax
from jax.experimental import pallas as pl
from jax.experimental.pallas import tpu as pltpu

D_MODEL = 1024
DEPTH = 4
N_A = DEPTH // 2
D_RNN = D_MODEL
LRU_BLOCK_W = 256
LRU_BLOCKS = D_RNN // LRU_BLOCK_W
CONV_W = 4
LRU_C = 8.0
N_HEADS = 16
HEAD_DIM = D_MODEL // N_HEADS
D_FF = 2816
EPS = 1e-6

SUBLANES = 8
LANES = 128
BF16_ROWS = 2 * SUBLANES
VMEM_LIMIT_BYTES = 56 * 1024 * 1024
TINY = 1e-30

F32 = jnp.float32
BF16 = jnp.bfloat16

ROW_TILE = 512
SEQ_TILE = 512
ATT_TILE = 256
FFN_CHUNKS = ((0, 512), (512, 512), (1024, 512), (1536, 512), (2048, 512), (2560, 256))


def _rmsnorm(x, g):
    y = x * lax.rsqrt(jnp.mean(x * x, axis=-1, keepdims=True) + EPS)
    return y * g


def _sigmoid(x):
    return 1.0 / (1.0 + jnp.exp(-x))


def _softplus(x):
    return jnp.maximum(x, 0.0) + jnp.log1p(jnp.exp(-jnp.abs(x)))


def _sigmoid_tanh(x):
    return 0.5 * jnp.tanh(0.5 * x) + 0.5


def _gelu_tanh(x):
    c = math.sqrt(2.0 / math.pi)
    half = 0.5 * x
    return half + half * jnp.tanh(x * (c + (c * 0.044715) * (x * x)))


def _dot(a, b):
    return jnp.dot(a, b, preferred_element_type=F32)


def _const_spec(shape):
    n = len(shape)
    return pl.BlockSpec(shape, lambda *_: (0,) * n, pipeline_mode=pl.Buffered(1))


def _layer_spec(shape, layer):
    n = len(shape)
    return pl.BlockSpec((None,) + tuple(shape), lambda *_: (layer,) + (0,) * n, pipeline_mode=pl.Buffered(1))


def _weight_spec(shape, layer):
    return _const_spec(shape) if layer is None else _layer_spec(shape, layer)


def _cast_job(w, steps, layer):
    n_layers, r, c = w.shape
    rt = r // steps
    assert rt * steps == r and rt % BF16_ROWS == 0, (w.shape, steps)
    if layer is None:
        return (pl.BlockSpec((n_layers, rt, c), lambda i: (0, i, 0)),
                pl.BlockSpec((n_layers, rt, c), lambda i: (0, i, 0)),
                jax.ShapeDtypeStruct(w.shape, BF16))
    return (pl.BlockSpec((None, rt, c), lambda i: (layer, i, 0)),
            pl.BlockSpec((rt, c), lambda i: (i, 0)),
            jax.ShapeDtypeStruct((r, c), BF16))


def _run_casts(srcs, dsts):
    for src, dst in zip(srcs, dsts):
        dst[...] = src[...].astype(BF16)


def _params(*sem):
    return pltpu.CompilerParams(dimension_semantics=sem, vmem_limit_bytes=VMEM_LIMIT_BYTES)


PAIR_W = 2 * LANES
N_PAIRS = N_HEADS // 2
QA_W = N_PAIRS * PAIR_W
C3_ONE = 3 * N_HEADS
N_PIECES = 3


def _split3(c):
    hi = c.astype(BF16).astype(F32)
    r = c - hi
    mid = r.astype(BF16).astype(F32)
    lo = (r - mid).astype(BF16).astype(F32)
    return hi, mid, lo


def _aug_placement():
    e = [[0.0] * D_MODEL for _ in range(LANES)]
    for h in range(N_HEADS):
        base = LANES * (h // 2) + HEAD_DIM * (h % 2)
        for piece in range(N_PIECES):
            e[C3_ONE][base + piece] = 1.0
            e[piece * N_HEADS + h][base + N_PIECES + piece] = 1.0
    return jnp.array(e, BF16)


def _emit_query(x, g, wq_ref, c3, e_ref, qa_ref):
    xn = _rmsnorm(x, g).astype(BF16)
    q = _dot(xn, wq_ref[...]) * (HEAD_DIM ** -0.5)
    aug = _dot(c3, e_ref[...])
    for p in range(N_PAIRS):
        qa_ref[:, PAIR_W * p:PAIR_W * p + LANES] = q[:, LANES * p:LANES * (p + 1)].astype(BF16)
        qa_ref[:, PAIR_W * p + LANES:PAIR_W * (p + 1)] = aug[:, LANES * p:LANES * (p + 1)].astype(BF16)


def _ffn_kernel(*refs, has_attn, has_final, has_query, n_casts):
    it = iter(refs)
    h_ref = next(it)
    if has_attn:
        o_ref, wo_ref = next(it), next(it)
    g_ref, win_ref, wout_ref = next(it), next(it), next(it)
    if has_final:
        gf_ref = next(it)
    if has_query:
        gq_ref, wq_ref, c3_ref, e_ref = next(it), next(it), next(it), next(it)
    cast_srcs = [next(it) for _ in range(n_casts)]
    out_ref = next(it)
    qa_ref = next(it) if has_query else None
    _run_casts(cast_srcs, [next(it) for _ in range(n_casts)])

    x = h_ref[...]
    if has_attn:
        x = x + _dot(o_ref[...], wo_ref[...])
    xn = _rmsnorm(x, g_ref[...]).astype(BF16)
    acc = x
    for c0, cw in FFN_CHUNKS:
        gate = _dot(xn, win_ref[:, c0:c0 + cw])
        up = _dot(xn, win_ref[:, D_FF + c0:D_FF + c0 + cw])
        act = (gate * _sigmoid(gate)) * up
        acc = acc + _dot(act.astype(BF16), wout_ref[c0:c0 + cw, :])
    if has_final:
        acc = _rmsnorm(acc, gf_ref[...])
    out_ref[...] = acc
    if has_query:
        _emit_query(acc, gq_ref[...], wq_ref, c3_ref[...], e_ref, qa_ref)


def _ffn(h2d, layer, norm_ffn, w_in, w_out, attn=None, final_g=None, query=None, casts=()):
    m = h2d.shape[0]
    tile = ROW_TILE
    row = pl.BlockSpec((tile, D_MODEL), lambda i: (i, 0))
    args = [h2d]
    specs = [row]
    if attn is not None:
        o2d, w_o, j = attn
        args += [o2d, w_o]
        specs += [row, _layer_spec((D_MODEL, D_MODEL), j)]
    args += [norm_ffn, w_in[0], w_out[0]]
    specs += [_layer_spec((1, D_MODEL), layer), _weight_spec((D_MODEL, 2 * D_FF), w_in[1]),
              _weight_spec((D_FF, D_MODEL), w_out[1])]
    if final_g is not None:
        args.append(final_g)
        specs.append(_const_spec((1, D_MODEL)))
    out_specs = row
    out_shape = jax.ShapeDtypeStruct((m, D_MODEL), F32)
    if query is not None:
        norm_mix, w_q, j, c3, e = query
        args += [norm_mix, w_q, c3, e]
        specs += [_layer_spec((1, D_MODEL), layer + 1), _layer_spec((D_MODEL, D_MODEL), j),
                  pl.BlockSpec((tile, LANES), lambda i: (i, 0)), _const_spec((LANES, D_MODEL))]
        out_specs = [row, pl.BlockSpec((tile, QA_W), lambda i: (i, 0))]
        out_shape = [out_shape, jax.ShapeDtypeStruct((m, QA_W), BF16)]
    if casts:
        jobs = [_cast_job(w, m // tile, lyr) for w, lyr in casts]
        args += [w for w, _ in casts]
        specs += [j[0] for j in jobs]
        out_specs = ([out_specs] if query is None else out_specs) + [j[1] for j in jobs]
        out_shape = ([out_shape] if query is None else out_shape) + [j[2] for j in jobs]
    return pl.pallas_call(
        functools.partial(_ffn_kernel, has_attn=attn is not None, has_final=final_g is not None,
                          has_query=query is not None, n_casts=len(casts)),
        grid=(m // tile,),
        in_specs=specs,
        out_specs=out_specs,
        out_shape=out_shape,
        compiler_params=_params("parallel"),
        name="ffn",
    )(*args)


REC_STEPS = 64
REC_ROWS = REC_STEPS * SUBLANES
HALO_ROWS = (CONV_W - 1) * SUBLANES


def _rec_kernel(*refs, n_casts):
    (h_ref, g_ref, win_ref, cw_ref, cb_ref, wg_ref, bg_ref, lam_ref, wout_ref), refs = refs[:9], refs[9:]
    cast_srcs, out_ref, cast_dsts = refs[:n_casts], refs[n_casts], refs[n_casts + 1:2 * n_casts + 1]
    x_scr, rec_scr, a_scr, u_scr, carry_scr = refs[2 * n_casts + 1:]
    rows = REC_ROWS
    _run_casts(cast_srcs, cast_dsts)

    @pl.when(pl.program_id(0) == 0)
    def _():
        rec_scr[0:HALO_ROWS, :] = jnp.zeros((HALO_ROWS, D_RNN), F32)
        carry_scr[...] = jnp.zeros((SUBLANES, D_RNN), F32)

    for t in range(REC_STEPS):
        x_scr[t * SUBLANES:(t + 1) * SUBLANES, :] = h_ref[:, t, :]
    x = x_scr[...]
    xn = _rmsnorm(x, g_ref[...]).astype(BF16)
    proj = _dot(xn, win_ref[...])
    rec_scr[HALO_ROWS:HALO_ROWS + rows, :] = proj[:, D_RNN:]

    conv = cb_ref[...]
    for tap in range(CONV_W):
        conv = conv + rec_scr[tap * SUBLANES:tap * SUBLANES + rows, :] * cw_ref[tap:tap + 1, :]
    rec_scr[0:HALO_ROWS, :] = rec_scr[rows:rows + HALO_ROWS, :]

    for n in range(LRU_BLOCKS):
        lo = n * LRU_BLOCK_W
        xb = conv[:, lo:lo + LRU_BLOCK_W]
        gates = _dot(xb.astype(BF16), wg_ref[n]) + bg_ref[n]
        gate_i = _sigmoid_tanh(gates[:, :LRU_BLOCK_W])
        gate_r = _sigmoid_tanh(gates[:, LRU_BLOCK_W:])
        log_a = gate_r * (-LRU_C * _softplus(-lam_ref[:, lo:lo + LRU_BLOCK_W]))
        a = jnp.exp(log_a)
        a_scr[:, lo:lo + LRU_BLOCK_W] = a
        one_m_a2 = 1.0 - a * a
        mult = one_m_a2 * lax.rsqrt(jnp.maximum(one_m_a2, TINY))
        u_scr[:, lo:lo + LRU_BLOCK_W] = xb * gate_i * mult

    hs = carry_scr[...]
    for t in range(REC_STEPS):
        r0 = t * SUBLANES
        hs = a_scr[r0:r0 + SUBLANES, :] * hs + u_scr[r0:r0 + SUBLANES, :]
        u_scr[r0:r0 + SUBLANES, :] = hs
    carry_scr[...] = hs
    y = _gelu_tanh(proj[:, :D_RNN]) * u_scr[...]
    x_scr[...] = x + _dot(y.astype(BF16), wout_ref[...])
    for t in range(REC_STEPS):
        out_ref[:, t, :] = x_scr[t * SUBLANES:(t + 1) * SUBLANES, :]


def _rec_layer(h, i, norm_mix, w_in, conv_w, conv_b, w_gates, b_gates, lam, w_out, casts=()):
    b, s, _ = h.shape
    assert b == SUBLANES, "the (time, batch) row layout needs one batch row per sublane"
    tile = pl.BlockSpec((b, REC_STEPS, D_MODEL), lambda si: (0, si, 0))
    rows = REC_ROWS
    jobs = [_cast_job(w, s // REC_STEPS, lyr) for w, lyr in casts]
    return pl.pallas_call(
        functools.partial(_rec_kernel, n_casts=len(casts)),
        grid=(s // REC_STEPS,),
        in_specs=[
            tile,
            _layer_spec((1, D_MODEL), i),
            _layer_spec((D_MODEL, 2 * D_RNN), i),
            _layer_spec((CONV_W, D_RNN), i),
            _layer_spec((1, D_RNN), i),
            _layer_spec((LRU_BLOCKS, LRU_BLOCK_W, 2 * LRU_BLOCK_W), i),
            _layer_spec((LRU_BLOCKS, 1, 2 * LRU_BLOCK_W), i),
            _layer_spec((1, D_RNN), i),
            _layer_spec((D_RNN, D_MODEL), i),
        ] + [j[0] for j in jobs],
        out_specs=[tile] + [j[1] for j in jobs],
        out_shape=[jax.ShapeDtypeStruct(h.shape, F32)] + [j[2] for j in jobs],
        scratch_shapes=[
            pltpu.VMEM((rows, D_MODEL), F32),
            pltpu.VMEM((rows + HALO_ROWS, D_RNN), F32),
            pltpu.VMEM((rows, D_RNN), F32),
            pltpu.VMEM((rows, D_RNN), F32),
            pltpu.VMEM((SUBLANES, D_RNN), F32),
        ],
        compiler_params=_params("arbitrary"),
        name="rec_layer",
    )(h, norm_mix, w_in, conv_w, conv_b, w_gates, b_gates, lam, w_out, *[w for w, _ in casts])


def _kvf_kernel(h_ref, g_ref, wkv_ref, wf_ref, bf_ref, gq_ref, wq_ref, e_ref,
                kt_ref, va_ref, c3_ref, qa_ref, carry_scr):
    ts = SEQ_TILE

    @pl.when(pl.program_id(1) == 0)
    def _():
        carry_scr[...] = jnp.zeros((1, LANES), F32)

    x = h_ref[0]
    xn = _rmsnorm(x, g_ref[...]).astype(BF16)
    kv = _dot(xn, wkv_ref[...])

    f_logit = _dot(xn, wf_ref[...]) + bf_ref[...]
    c = -_softplus(-f_logit)
    row = lax.broadcasted_iota(jnp.int32, (ts, LANES), 0)
    k = 1
    while k < ts:
        c = c + jnp.where(row >= k, pltpu.roll(c, k, 0), 0.0)
        k *= 2
    c = c + carry_scr[...]
    carry_scr[...] = c[ts - 1:ts, :]

    lane = lax.broadcasted_iota(jnp.int32, (ts, LANES), 1)
    hi, mid, lo = _split3(c)
    c3 = jnp.where(lane < N_HEADS, hi,
                   jnp.where(lane < 2 * N_HEADS, pltpu.roll(mid, N_HEADS, 1),
                             jnp.where(lane < C3_ONE, pltpu.roll(lo, 2 * N_HEADS, 1),
                                       jnp.where(lane == C3_ONE, 1.0, 0.0)))).astype(BF16)
    c3_ref[0] = c3
    _emit_query(x, gq_ref[...], wq_ref, c3, e_ref, qa_ref.at[0])

    c_t = c.T
    ones_tile = jnp.where(lane == 0, 1.0, 0.0).astype(BF16)
    trow = lax.broadcasted_iota(jnp.int32, (LANES, ts), 0)
    arow = lax.broadcasted_iota(jnp.int32, (BF16_ROWS, ts), 0)
    for p in range(N_PAIRS):
        va_ref[0, p, :, 0:LANES] = kv[:, D_MODEL + LANES * p:D_MODEL + LANES * (p + 1)].astype(BF16)
        va_ref[0, p, :, LANES:PAIR_W] = ones_tile
        k_t = kv[:, LANES * p:LANES * (p + 1)].T
        for j in range(2):
            h = 2 * p + j
            own = (trow >= HEAD_DIM * j) & (trow < HEAD_DIM * (j + 1))
            kt_ref[0, h, 0:LANES, :] = jnp.where(own, k_t, 0.0).astype(BF16)
            chi, cmid, clo = _split3(c_t[h:h + 1, :])
            aug = jnp.where(arow == 0, -chi, jnp.where(arow == 1, -cmid, jnp.where(arow == 2, -clo,
                            jnp.where(arow < 2 * N_PIECES, 1.0, 0.0))))
            kt_ref[0, h, LANES:PAIR_W, :] = jnp.zeros((LANES, ts), BF16)
            a0 = LANES + HEAD_DIM * j
            kt_ref[0, h, a0:a0 + BF16_ROWS, :] = aug.astype(BF16)


def _kvf(h, norm_kv, w_kv, w_f, b_f, norm_mix, w_q, e):
    b, s, _ = h.shape
    ts = SEQ_TILE
    return pl.pallas_call(
        _kvf_kernel,
        grid=(b, s // ts),
        in_specs=[
            pl.BlockSpec((1, ts, D_MODEL), lambda bi, si: (bi, si, 0)),
            _const_spec((1, D_MODEL)),
            _const_spec((D_MODEL, 2 * D_MODEL)),
            _const_spec((D_MODEL, LANES)),
            _const_spec((1, LANES)),
            _layer_spec((1, D_MODEL), N_A),
            _layer_spec((D_MODEL, D_MODEL), 0),
            _const_spec((LANES, D_MODEL)),
        ],
        out_specs=[
            pl.BlockSpec((1, N_HEADS, PAIR_W, ts), lambda bi, si: (bi, 0, 0, si)),
            pl.BlockSpec((1, N_PAIRS, ts, PAIR_W), lambda bi, si: (bi, 0, si, 0)),
            pl.BlockSpec((1, ts, LANES), lambda bi, si: (bi, si, 0)),
            pl.BlockSpec((1, ts, QA_W), lambda bi, si: (bi, si, 0)),
        ],
        out_shape=[
            jax.ShapeDtypeStruct((b, N_HEADS, PAIR_W, s), BF16),
            jax.ShapeDtypeStruct((b, N_PAIRS, s, PAIR_W), BF16),
            jax.ShapeDtypeStruct((b, s, LANES), BF16),
            jax.ShapeDtypeStruct((b, s, QA_W), BF16),
        ],
        scratch_shapes=[pltpu.VMEM((1, LANES), F32)],
        compiler_params=_params("parallel", "arbitrary"),
        name="kvf",
    )(h, norm_kv, w_kv, w_f, b_f, norm_mix, w_q, e)


def _attn_kernel(qa_ref, kt_ref, va_ref, o_ref):
    t = ATT_TILE
    s_len = qa_ref.shape[1]
    tri = (lax.broadcasted_iota(jnp.int32, (t, t), 0) >= lax.broadcasted_iota(jnp.int32, (t, t), 1))
    lane = lax.broadcasted_iota(jnp.int32, (t, LANES), 1)

    def logits(i, j):
        lo = i * t
        s = _dot(qa_ref[0, lo:lo + t, :], kt_ref[0, j, :, 0:lo + t])
        s_diag = jnp.where(tri, s[:, lo:], -jnp.inf)
        s = jnp.concatenate([s[:, :lo], s_diag], axis=1) if i > 0 else s_diag
        return s, jnp.max(s, axis=-1, keepdims=True)

    def weighted_values(i, s, m):
        acc = _dot(jnp.exp(s - m).astype(BF16), va_ref[0, 0, 0:(i + 1) * t, :])
        return acc[:, :LANES] * (1.0 / acc[:, LANES:LANES + 1])

    units = [(i, j) for i in range(s_len // t) for j in range(2)]
    pending = logits(*units[0])
    even = None
    for n, (i, j) in enumerate(units):
        ahead = logits(*units[n + 1]) if n + 1 < len(units) else None
        out = weighted_values(i, *pending)
        pending = ahead
        if j == 0:
            even = out
        else:
            o_ref[0, i * t:(i + 1) * t, :] = jnp.where(lane < HEAD_DIM, even, out).astype(BF16)


def _attention(qa, kt, va):
    b, s, _ = qa.shape
    return pl.pallas_call(
        _attn_kernel,
        grid=(b, N_PAIRS),
        in_specs=[
            pl.BlockSpec((1, s, PAIR_W), lambda bi, p: (bi, 0, p)),
            pl.BlockSpec((1, 2, PAIR_W, s), lambda bi, p: (bi, p, 0, 0)),
            pl.BlockSpec((1, 1, s, PAIR_W), lambda bi, p: (bi, p, 0, 0)),
        ],
        out_specs=pl.BlockSpec((1, s, LANES), lambda bi, p: (bi, 0, p)),
        out_shape=jax.ShapeDtypeStruct((b, s, D_MODEL), BF16),
        compiler_params=_params("parallel", "parallel"),
        name="fox_attention",
    )(qa, kt, va)


def kernel(x, norm_mix, norm_ffn, w_ffn_in, w_ffn_out, w_rec_in, conv_w, conv_b, w_lru_gates,
           b_lru_gates, lru_param, w_rec_out, norm_kv, w_kvf, b_forget, w_q, w_o, norm_final):
    b, s, d = x.shape
    m = b * s
    row = lambda a: a.reshape(1, -1)
    rows = lambda a: a.reshape(a.shape[0], 1, a.shape[1])

    norm_mix, norm_ffn = rows(norm_mix), rows(norm_ffn)
    w_ffn_out = w_ffn_out.reshape(DEPTH, D_FF * D_MODEL // (2 * D_FF), 2 * D_FF)
    w_rec_in, w_rec_out = w_rec_in.astype(BF16), w_rec_out.astype(BF16)
    w_lru_gates = w_lru_gates.astype(BF16)
    b_lru_gates = b_lru_gates.reshape(N_A, LRU_BLOCKS, 1, 2 * LRU_BLOCK_W)
    conv_b, lru_param = rows(conv_b), rows(lru_param)
    w_q, w_o = w_q.astype(BF16), w_o.astype(BF16)

    h = x
    for i in range(N_A):
        rec = _rec_layer(h, i, norm_mix, w_rec_in, conv_w, conv_b, w_lru_gates, b_lru_gates, lru_param, w_rec_out,
                         casts=((w_ffn_in, 0), (w_ffn_out, 0)) if i == 0 else ())
        if i == 0:
            h, fin0, fout0 = rec
            h, fin, fout = _ffn(h.reshape(m, d), i, norm_ffn, (fin0, None), (fout0.reshape(D_FF, D_MODEL), None),
                                casts=((w_ffn_in, None), (w_ffn_out, None)))
            w_ffn_in, w_ffn_out = fin, fout.reshape(DEPTH, D_FF, D_MODEL)
        else:
            h = _ffn(rec[0].reshape(m, d), i, norm_ffn, (w_ffn_in, i), (w_ffn_out, i))
        h = h.reshape(b, s, d)

    w_kv = w_kvf[:, :2 * D_MODEL].astype(BF16)
    w_f = jnp.pad(w_kvf[:, 2 * D_MODEL:], ((0, 0), (0, LANES - N_HEADS))).astype(BF16)
    b_f = jnp.pad(b_forget, (0, LANES - N_HEADS)).reshape(1, LANES)
    e = _aug_placement()
    kt, va, c3, qa = _kvf(h, row(norm_kv), w_kv, w_f, b_f, norm_mix, w_q, e)
    c3 = c3.reshape(m, LANES)

    h2d = h.reshape(m, d)
    for j in range(DEPTH - N_A):
        layer = N_A + j
        o = _attention(qa, kt, va)
        attn = (o.reshape(m, d), w_o, j)
        if layer == DEPTH - 1:
            h2d = _ffn(h2d, layer, norm_ffn, (w_ffn_in, layer), (w_ffn_out, layer), attn=attn,
                       final_g=row(norm_final))
        else:
            h2d, qa = _ffn(h2d, layer, norm_ffn, (w_ffn_in, layer), (w_ffn_out, layer), attn=attn,
                           query=(norm_mix, w_q, j + 1, c3, e))
            qa = qa.reshape(b, s, QA_W)
    return h2d.reshape(b, s, d)
```

```python
import functools
import math

import jax
import jax.numpy as jnp
from jax import lax
from jax.experimental import pallas as pl
from jax.experimental.pallas import tpu as pltpu

D_MODEL = 1024
DEPTH = 4
N_A = DEPTH // 2
D_RNN = D_MODEL
LRU_BLOCK_W = 256
LRU_BLOCKS = D_RNN // LRU_BLOCK_W
CONV_W = 4
LRU_C = 8.0
N_HEADS = 16
HEAD_DIM = D_MODEL // N_HEADS
D_FF = 2816
EPS = 1e-6

SUBLANES = 8
LANES = 128
BF16_ROWS = 2 * SUBLANES
VMEM_LIMIT_BYTES = 56 * 1024 * 1024
TINY = 1e-30

F32 = jnp.float32
BF16 = jnp.bfloat16

ROW_TILE = 512
SEQ_TILE = 512
ATT_TILE = 256
FFN_CHUNKS = ((0, 512), (512, 512), (1024, 512), (1536, 512), (2048, 512), (2560, 256))


def _rmsnorm(x, g):
    y = x * lax.rsqrt(jnp.mean(x * x, axis=-1, keepdims=True) + EPS)
    return y * g


def _sigmoid(x):
    return 1.0 / (1.0 + jnp.exp(-x))


def _softplus(x):
    return jnp.maximum(x, 0.0) + jnp.log1p(jnp.exp(-jnp.abs(x)))


def _sigmoid_of_twice(half_x):
    return 0.5 * jnp.tanh(half_x) + 0.5


def _gelu_tanh(x):
    c = math.sqrt(2.0 / math.pi)
    half = 0.5 * x
    return half + half * jnp.tanh(x * (c + (c * 0.044715) * (x * x)))


def _dot(a, b):
    return jnp.dot(a, b, preferred_element_type=F32)


def _const_spec(shape):
    n = len(shape)
    return pl.BlockSpec(shape, lambda *_: (0,) * n, pipeline_mode=pl.Buffered(1))


def _layer_spec(shape, layer):
    n = len(shape)
    return pl.BlockSpec((None,) + tuple(shape), lambda *_: (layer,) + (0,) * n, pipeline_mode=pl.Buffered(1))


def _weight_spec(shape, layer):
    return _const_spec(shape) if layer is None else _layer_spec(shape, layer)


def _cast_job(w, steps, layer):
    n_layers, r, c = w.shape
    hold = next(k for k in (1, 2, 4, 8) if r % (steps // k) == 0 and (r // (steps // k)) % BF16_ROWS == 0)
    rt = r // (steps // hold)
    if layer is None:
        return (pl.BlockSpec((n_layers, rt, c), lambda i: (0, i // hold, 0)),
                pl.BlockSpec((n_layers, rt, c), lambda i: (0, i // hold, 0)),
                jax.ShapeDtypeStruct(w.shape, BF16))
    return (pl.BlockSpec((None, rt, c), lambda i: (layer, i // hold, 0)),
            pl.BlockSpec((rt, c), lambda i: (i // hold, 0)),
            jax.ShapeDtypeStruct((r, c), BF16))


def _run_casts(srcs, dsts):
    for src, dst in zip(srcs, dsts):
        dst[...] = src[...].astype(BF16)


def _params(*sem):
    return pltpu.CompilerParams(dimension_semantics=sem, vmem_limit_bytes=VMEM_LIMIT_BYTES)


PAIR_W = 2 * LANES
N_PAIRS = N_HEADS // 2
QA_W = N_PAIRS * PAIR_W
C3_ONE = 3 * N_HEADS
N_PIECES = 3


def _split3(c):
    hi = c.astype(BF16).astype(F32)
    r = c - hi
    mid = r.astype(BF16).astype(F32)
    lo = (r - mid).astype(BF16).astype(F32)
    return hi, mid, lo


def _aug_placement():
    e = [[0.0] * D_MODEL for _ in range(LANES)]
    for h in range(N_HEADS):
        base = LANES * (h // 2) + HEAD_DIM * (h % 2)
        for piece in range(N_PIECES):
            e[C3_ONE][base + piece] = 1.0
            e[piece * N_HEADS + h][base + N_PIECES + piece] = 1.0
    return jnp.array(e, BF16)


def _emit_query(x, g, wq_ref, c3, e_ref, qa_ref):
    xn = _rmsnorm(x, g).astype(BF16)
    q = _dot(xn, wq_ref[...]) * (HEAD_DIM ** -0.5)
    aug = _dot(c3, e_ref[...])
    for p in range(N_PAIRS):
        qa_ref[:, PAIR_W * p:PAIR_W * p + LANES] = q[:, LANES * p:LANES * (p + 1)].astype(BF16)
        qa_ref[:, PAIR_W * p + LANES:PAIR_W * (p + 1)] = aug[:, LANES * p:LANES * (p + 1)].astype(BF16)


def _ffn_kernel(*refs, has_attn, has_final, has_query, n_casts):
    it = iter(refs)
    h_ref = next(it)
    if has_attn:
        o_ref, wo_ref = next(it), next(it)
    g_ref, win_ref, wout_ref = next(it), next(it), next(it)
    if has_final:
        gf_ref = next(it)
    if has_query:
        gq_ref, wq_ref, c3_ref, e_ref = next(it), next(it), next(it), next(it)
    cast_srcs = [next(it) for _ in range(n_casts)]
    out_ref = next(it)
    qa_ref = next(it) if has_query else None
    _run_casts(cast_srcs, [next(it) for _ in range(n_casts)])

    x = h_ref[...]
    if has_attn:
        x = x + _dot(o_ref[...], wo_ref[...])
    xn = _rmsnorm(x, g_ref[...]).astype(BF16)
    acc = x
    for c0, cw in FFN_CHUNKS:
        gate = _dot(xn, win_ref[:, c0:c0 + cw])
        up = _dot(xn, win_ref[:, D_FF + c0:D_FF + c0 + cw])
        act = (gate * _sigmoid(gate)) * up
        acc = acc + _dot(act.astype(BF16), wout_ref[c0:c0 + cw, :])
    if has_final:
        acc = _rmsnorm(acc, gf_ref[...])
    out_ref[...] = acc
    if has_query:
        _emit_query(acc, gq_ref[...], wq_ref, c3_ref[...], e_ref, qa_ref)


def _ffn(h2d, layer, norm_ffn, w_in, w_out, attn=None, final_g=None, query=None, casts=()):
    m = h2d.shape[0]
    tile = ROW_TILE
    row = pl.BlockSpec((tile, D_MODEL), lambda i: (i, 0))
    args = [h2d]
    specs = [row]
    if attn is not None:
        o2d, w_o, j = attn
        args += [o2d, w_o]
        specs += [row, _layer_spec((D_MODEL, D_MODEL), j)]
    args += [norm_ffn, w_in[0], w_out[0]]
    specs += [_layer_spec((1, D_MODEL), layer), _weight_spec((D_MODEL, 2 * D_FF), w_in[1]),
              _weight_spec((D_FF, D_MODEL), w_out[1])]
    if final_g is not None:
        args.append(final_g)
        specs.append(_const_spec((1, D_MODEL)))
    out_specs = row
    out_shape = jax.ShapeDtypeStruct((m, D_MODEL), F32)
    if query is not None:
        norm_mix, w_q, j, c3, e = query
        args += [norm_mix, w_q, c3, e]
        specs += [_layer_spec((1, D_MODEL), layer + 1), _layer_spec((D_MODEL, D_MODEL), j),
                  pl.BlockSpec((tile, LANES), lambda i: (i, 0)), _const_spec((LANES, D_MODEL))]
        out_specs = [row, pl.BlockSpec((tile, QA_W), lambda i: (i, 0))]
        out_shape = [out_shape, jax.ShapeDtypeStruct((m, QA_W), BF16)]
    if casts:
        jobs = [_cast_job(w, m // tile, lyr) for w, lyr in casts]
        args += [w for w, _ in casts]
        specs += [j[0] for j in jobs]
        out_specs = ([out_specs] if query is None else out_specs) + [j[1] for j in jobs]
        out_shape = ([out_shape] if query is None else out_shape) + [j[2] for j in jobs]
    return pl.pallas_call(
        functools.partial(_ffn_kernel, has_attn=attn is not None, has_final=final_g is not None,
                          has_query=query is not None, n_casts=len(casts)),
        grid=(m // tile,),
        in_specs=specs,
        out_specs=out_specs,
        out_shape=out_shape,
        compiler_params=_params("parallel"),
        name="ffn",
    )(*args)


REC_STEPS = 64
REC_ROWS = REC_STEPS * SUBLANES
HALO_ROWS = (CONV_W - 1) * SUBLANES


def _rec_kernel(*refs, n_casts):
    (h_ref, g_ref, win_ref, cw_ref, cb_ref, wg_ref, bg_ref, lam_ref, wout_ref), refs = refs[:9], refs[9:]
    cast_srcs, out_ref, cast_dsts = refs[:n_casts], refs[n_casts], refs[n_casts + 1:2 * n_casts + 1]
    x_scr, rec_scr, a_scr, u_scr, carry_scr = refs[2 * n_casts + 1:]
    rows = REC_ROWS
    _run_casts(cast_srcs, cast_dsts)

    @pl.when(pl.program_id(0) == 0)
    def _():
        rec_scr[0:HALO_ROWS, :] = jnp.zeros((HALO_ROWS, D_RNN), F32)
        carry_scr[...] = jnp.zeros((SUBLANES, D_RNN), F32)

    for t in range(REC_STEPS):
        x_scr[t * SUBLANES:(t + 1) * SUBLANES, :] = h_ref[:, t, :]
    x = x_scr[...]
    xn = _rmsnorm(x, g_ref[...]).astype(BF16)
    proj = _dot(xn, win_ref[...])
    rec_scr[HALO_ROWS:HALO_ROWS + rows, :] = proj[:, D_RNN:]

    conv = cb_ref[...]
    for tap in range(CONV_W):
        conv = conv + rec_scr[tap * SUBLANES:tap * SUBLANES + rows, :] * cw_ref[tap:tap + 1, :]
    rec_scr[0:HALO_ROWS, :] = rec_scr[rows:rows + HALO_ROWS, :]

    for n in range(LRU_BLOCKS):
        lo = n * LRU_BLOCK_W
        xb = conv[:, lo:lo + LRU_BLOCK_W]
        half_gates = _dot(xb.astype(BF16), wg_ref[n]) + bg_ref[n]
        gate_i = _sigmoid_of_twice(half_gates[:, :LRU_BLOCK_W])
        gate_r = _sigmoid_of_twice(half_gates[:, LRU_BLOCK_W:])
        log_a = gate_r * (-LRU_C * _softplus(-lam_ref[:, lo:lo + LRU_BLOCK_W]))
        a = jnp.exp(log_a)
        a_scr[:, lo:lo + LRU_BLOCK_W] = a
        one_m_a2 = 1.0 - a * a
        mult = one_m_a2 * lax.rsqrt(jnp.maximum(one_m_a2, TINY))
        u_scr[:, lo:lo + LRU_BLOCK_W] = xb * gate_i * mult

    hs = carry_scr[...]
    for t in range(REC_STEPS):
        r0 = t * SUBLANES
        hs = a_scr[r0:r0 + SUBLANES, :] * hs + u_scr[r0:r0 + SUBLANES, :]
        u_scr[r0:r0 + SUBLANES, :] = hs
    carry_scr[...] = hs
    y = _gelu_tanh(proj[:, :D_RNN]) * u_scr[...]
    x_scr[...] = x + _dot(y.astype(BF16), wout_ref[...])
    for t in range(REC_STEPS):
        out_ref[:, t, :] = x_scr[t * SUBLANES:(t + 1) * SUBLANES, :]


def _rec_layer(h, i, norm_mix, w_in, conv_w, conv_b, w_gates, b_gates, lam, w_out, casts=()):
    b, s, _ = h.shape
    assert b == SUBLANES, "the (time, batch) row layout needs one batch row per sublane"
    tile = pl.BlockSpec((b, REC_STEPS, D_MODEL), lambda si: (0, si, 0))
    rows = REC_ROWS
    jobs = [_cast_job(w, s // REC_STEPS, lyr) for w, lyr in casts]
    return pl.pallas_call(
        functools.partial(_rec_kernel, n_casts=len(casts)),
        grid=(s // REC_STEPS,),
        in_specs=[
            tile,
            _layer_spec((1, D_MODEL), i),
            _layer_spec((D_MODEL, 2 * D_RNN), i),
            _layer_spec((CONV_W, D_RNN), i),
            _layer_spec((1, D_RNN), i),
            _layer_spec((LRU_BLOCKS, LRU_BLOCK_W, 2 * LRU_BLOCK_W), i),
            _layer_spec((LRU_BLOCKS, 1, 2 * LRU_BLOCK_W), i),
            _layer_spec((1, D_RNN), i),
            _layer_spec((D_RNN, D_MODEL), i),
        ] + [j[0] for j in jobs],
        out_specs=[tile] + [j[1] for j in jobs],
        out_shape=[jax.ShapeDtypeStruct(h.shape, F32)] + [j[2] for j in jobs],
        scratch_shapes=[
            pltpu.VMEM((rows, D_MODEL), F32),
            pltpu.VMEM((rows + HALO_ROWS, D_RNN), F32),
            pltpu.VMEM((rows, D_RNN), F32),
            pltpu.VMEM((rows, D_RNN), F32),
            pltpu.VMEM((SUBLANES, D_RNN), F32),
        ],
        compiler_params=_params("arbitrary"),
        name="rec_layer",
    )(h, norm_mix, w_in, conv_w, conv_b, w_gates, b_gates, lam, w_out, *[w for w, _ in casts])


def _kvf_kernel(h_ref, g_ref, wkv_ref, wf_ref, bf_ref, gq_ref, wq_ref, e_ref,
                kt_ref, va_ref, c3_ref, qa_ref, carry_scr):
    ts = SEQ_TILE

    @pl.when(pl.program_id(1) == 0)
    def _():
        carry_scr[...] = jnp.zeros((1, LANES), F32)

    x = h_ref[0]
    xn = _rmsnorm(x, g_ref[...]).astype(BF16)
    kv = _dot(xn, wkv_ref[...])

    f_logit = _dot(xn, wf_ref[...]) + bf_ref[...]
    c = -_softplus(-f_logit)
    row = lax.broadcasted_iota(jnp.int32, (ts, LANES), 0)
    k = 1
    while k < ts:
        c = c + jnp.where(row >= k, pltpu.roll(c, k, 0), 0.0)
        k *= 2
    c = c + carry_scr[...]
    carry_scr[...] = c[ts - 1:ts, :]

    lane = lax.broadcasted_iota(jnp.int32, (ts, LANES), 1)
    hi, mid, lo = _split3(c)
    c3 = jnp.where(lane < N_HEADS, hi,
                   jnp.where(lane < 2 * N_HEADS, pltpu.roll(mid, N_HEADS, 1),
                             jnp.where(lane < C3_ONE, pltpu.roll(lo, 2 * N_HEADS, 1),
                                       jnp.where(lane == C3_ONE, 1.0, 0.0)))).astype(BF16)
    c3_ref[0] = c3
    _emit_query(x, gq_ref[...], wq_ref, c3, e_ref, qa_ref.at[0])

    c_t = c.T
    ones_tile = jnp.where(lane == 0, 1.0, 0.0).astype(BF16)
    trow = lax.broadcasted_iota(jnp.int32, (LANES, ts), 0)
    arow = lax.broadcasted_iota(jnp.int32, (BF16_ROWS, ts), 0)
    for p in range(N_PAIRS):
        va_ref[0, p, :, 0:LANES] = kv[:, D_MODEL + LANES * p:D_MODEL + LANES * (p + 1)].astype(BF16)
        va_ref[0, p, :, LANES:PAIR_W] = ones_tile
        k_t = kv[:, LANES * p:LANES * (p + 1)].T
        for j in range(2):
            h = 2 * p + j
            own = (trow >= HEAD_DIM * j) & (trow < HEAD_DIM * (j + 1))
            kt_ref[0, h, 0:LANES, :] = jnp.where(own, k_t, 0.0).astype(BF16)
            chi, cmid, clo = _split3(c_t[h:h + 1, :])
            aug = jnp.where(arow == 0, -chi, jnp.where(arow == 1, -cmid, jnp.where(arow == 2, -clo,
                            jnp.where(arow < 2 * N_PIECES, 1.0, 0.0))))
            kt_ref[0, h, LANES:PAIR_W, :] = jnp.zeros((LANES, ts), BF16)
            a0 = LANES + HEAD_DIM * j
            kt_ref[0, h, a0:a0 + BF16_ROWS, :] = aug.astype(BF16)


def _kvf(h, norm_kv, w_kv, w_f, b_f, norm_mix, w_q, e):
    b, s, _ = h.shape
    ts = SEQ_TILE
    return pl.pallas_call(
        _kvf_kernel,
        grid=(b, s // ts),
        in_specs=[
            pl.BlockSpec((1, ts, D_MODEL), lambda bi, si: (bi, si, 0)),
            _const_spec((1, D_MODEL)),
            _const_spec((D_MODEL, 2 * D_MODEL)),
            _const_spec((D_MODEL, LANES)),
            _const_spec((1, LANES)),
            _layer_spec((1, D_MODEL), N_A),
            _layer_spec((D_MODEL, D_MODEL), 0),
            _const_spec((LANES, D_MODEL)),
        ],
        out_specs=[
            pl.BlockSpec((1, N_HEADS, PAIR_W, ts), lambda bi, si: (bi, 0, 0, si)),
            pl.BlockSpec((1, N_PAIRS, ts, PAIR_W), lambda bi, si: (bi, 0, si, 0)),
            pl.BlockSpec((1, ts, LANES), lambda bi, si: (bi, si, 0)),
            pl.BlockSpec((1, ts, QA_W), lambda bi, si: (bi, si, 0)),
        ],
        out_shape=[
            jax.ShapeDtypeStruct((b, N_HEADS, PAIR_W, s), BF16),
            jax.ShapeDtypeStruct((b, N_PAIRS, s, PAIR_W), BF16),
            jax.ShapeDtypeStruct((b, s, LANES), BF16),
            jax.ShapeDtypeStruct((b, s, QA_W), BF16),
        ],
        scratch_shapes=[pltpu.VMEM((1, LANES), F32)],
        compiler_params=_params("parallel", "arbitrary"),
        name="kvf",
    )(h, norm_kv, w_kv, w_f, b_f, norm_mix, w_q, e)


def _attn_kernel(qa_ref, kt_ref, va_ref, o_ref):
    t = ATT_TILE
    s_len = qa_ref.shape[1]
    tri = (lax.broadcasted_iota(jnp.int32, (t, t), 0) >= lax.broadcasted_iota(jnp.int32, (t, t), 1))
    lane = lax.broadcasted_iota(jnp.int32, (t, LANES), 1)

    def logits(i, j):
        lo = i * t
        s = _dot(qa_ref[0, lo:lo + t, :], kt_ref[0, j, :, 0:lo + t])
        s_diag = jnp.where(tri, s[:, lo:], -jnp.inf)
        s = jnp.concatenate([s[:, :lo], s_diag], axis=1) if i > 0 else s_diag
        return s, jnp.max(s, axis=-1, keepdims=True)

    def weighted_values(i, s, m):
        acc = _dot(jnp.exp(s - m).astype(BF16), va_ref[0, 0, 0:(i + 1) * t, :])
        return acc[:, :LANES] * (1.0 / acc[:, LANES:LANES + 1])

    units = [(i, j) for i in range(s_len // t) for j in range(2)]
    pending = logits(*units[0])
    even = None
    for n, (i, j) in enumerate(units):
        ahead = logits(*units[n + 1]) if n + 1 < len(units) else None
        out = weighted_values(i, *pending)
        pending = ahead
        if j == 0:
            even = out
        else:
            o_ref[0, i * t:(i + 1) * t, :] = jnp.where(lane < HEAD_DIM, even, out).astype(BF16)


def _attention(qa, kt, va):
    b, s, _ = qa.shape
    return pl.pallas_call(
        _attn_kernel,
        grid=(b, N_PAIRS),
        in_specs=[
            pl.BlockSpec((1, s, PAIR_W), lambda bi, p: (bi, 0, p)),
            pl.BlockSpec((1, 2, PAIR_W, s), lambda bi, p: (bi, p, 0, 0)),
            pl.BlockSpec((1, 1, s, PAIR_W), lambda bi, p: (bi, p, 0, 0)),
        ],
        out_specs=pl.BlockSpec((1, s, LANES), lambda bi, p: (bi, 0, p)),
        out_shape=jax.ShapeDtypeStruct((b, s, D_MODEL), BF16),
        compiler_params=_params("parallel", "parallel"),
        name="fox_attention",
    )(qa, kt, va)


def kernel(x, norm_mix, norm_ffn, w_ffn_in, w_ffn_out, w_rec_in, conv_w, conv_b, w_lru_gates,
           b_lru_gates, lru_param, w_rec_out, norm_kv, w_kvf, b_forget, w_q, w_o, norm_final):
    b, s, d = x.shape
    m = b * s
    row = lambda a: a.reshape(1, -1)
    rows = lambda a: a.reshape(a.shape[0], 1, a.shape[1])

    norm_mix, norm_ffn = rows(norm_mix), rows(norm_ffn)
    w_rec_in, w_rec_out = w_rec_in.astype(BF16), w_rec_out.astype(BF16)
    w_lru_gates = (0.5 * w_lru_gates).astype(BF16)
    b_lru_gates = (0.5 * b_lru_gates).reshape(N_A, LRU_BLOCKS, 1, 2 * LRU_BLOCK_W)
    conv_b, lru_param = rows(conv_b), rows(lru_param)
    w_q, w_o = w_q.astype(BF16), w_o.astype(BF16)

    h = x
    for i in range(N_A):
        rec = _rec_layer(h, i, norm_mix, w_rec_in, conv_w, conv_b, w_lru_gates, b_lru_gates, lru_param, w_rec_out,
                         casts=((w_ffn_in, 0), (w_ffn_out, 0)) if i == 0 else ())
        if i == 0:
            h, fin0, fout0 = rec
            h, w_ffn_in, w_ffn_out = _ffn(h.reshape(m, d), i, norm_ffn, (fin0, None), (fout0, None),
                                          casts=((w_ffn_in, None), (w_ffn_out, None)))
        else:
            h = _ffn(rec[0].reshape(m, d), i, norm_ffn, (w_ffn_in, i), (w_ffn_out, i))
        h = h.reshape(b, s, d)

    w_kv = w_kvf[:, :2 * D_MODEL].astype(BF16)
    w_f = jnp.pad(w_kvf[:, 2 * D_MODEL:], ((0, 0), (0, LANES - N_HEADS))).astype(BF16)
    b_f = jnp.pad(b_forget, (0, LANES - N_HEADS)).reshape(1, LANES)
    e = _aug_placement()
    kt, va, c3, qa = _kvf(h, row(norm_kv), w_kv, w_f, b_f, norm_mix, w_q, e)
    c3 = c3.reshape(m, LANES)

    h2d = h.reshape(m, d)
    for j in range(DEPTH - N_A):
        layer = N_A + j
        o = _attention(qa, kt, va)
        attn = (o.reshape(m, d), w_o, j)
        if layer == DEPTH - 1:
            h2d = _ffn(h2d, layer, norm_ffn, (w_ffn_in, layer), (w_ffn_out, layer), attn=attn,
                       final_g=row(norm_final))
        else:
            h2d, qa = _ffn(h2d, layer, norm_ffn, (w_ffn_in, layer), (w_ffn_out, layer), attn=attn,
                           query=(norm_mix, w_q, j + 1, c3, e))
            qa = qa.reshape(b, s, QA_W)
    return h2d.reshape(b, s, d)
```

```python
import functools
import math

import jax
import jax.numpy as jnp
from jax import lax
from jax.experimental import pallas as pl
from jax.experimental.pallas import tpu as pltpu

D_MODEL = 1024
DEPTH = 4
N_A = DEPTH // 2
D_RNN = D_MODEL
LRU_BLOCK_W = 256
LRU_BLOCKS = D_RNN // LRU_BLOCK_W
CONV_W = 4
LRU_C = 8.0
N_HEADS = 16
HEAD_DIM = D_MODEL // N_HEADS
D_FF = 2816
EPS = 1e-6

SUBLANES = 8
LANES = 128
BF16_ROWS = 2 * SUBLANES
VMEM_LIMIT_BYTES = 56 * 1024 * 1024
TINY = 1e-30

F32 = jnp.float32
BF16 = jnp.bfloat16

ROW_TILE = 512
SEQ_TILE = 512
ATT_TILE = 256
FFN_CHUNKS = ((0, 512), (512, 512), (1024, 512), (1536, 512), (2048, 512), (2560, 256))


def _rmsnorm(x, g):
    y = x * lax.rsqrt(jnp.mean(x * x, axis=-1, keepdims=True) + EPS)
    return y * g


def _sigmoid(x):
    return 1.0 / (1.0 + jnp.exp(-x))


def _softplus(x):
    return jnp.maximum(x, 0.0) + jnp.log1p(jnp.exp(-jnp.abs(x)))


def _sigmoid_of_twice(half_x):
    return 0.5 * jnp.tanh(half_x) + 0.5


def _gelu_tanh(x):
    c = math.sqrt(2.0 / math.pi)
    half = 0.5 * x
    return half + half * jnp.tanh(x * (c + (c * 0.044715) * (x * x)))


def _dot(a, b):
    return jnp.dot(a, b, preferred_element_type=F32)


def _const_spec(shape):
    n = len(shape)
    return pl.BlockSpec(shape, lambda *_: (0,) * n, pipeline_mode=pl.Buffered(1))


def _layer_spec(shape, layer):
    n = len(shape)
    return pl.BlockSpec((None,) + tuple(shape), lambda *_: (layer,) + (0,) * n, pipeline_mode=pl.Buffered(1))


def _weight_spec(shape, layer):
    return _const_spec(shape) if layer is None else _layer_spec(shape, layer)


def _cast_job(w, steps, layer):
    n_layers, r, c = w.shape
    hold = next(k for k in (1, 2, 4, 8) if r % (steps // k) == 0 and (r // (steps // k)) % BF16_ROWS == 0)
    rt = r // (steps // hold)
    if layer is None:
        return (pl.BlockSpec((n_layers, rt, c), lambda i: (0, i // hold, 0)),
                pl.BlockSpec((n_layers, rt, c), lambda i: (0, i // hold, 0)),
                jax.ShapeDtypeStruct(w.shape, BF16))
    return (pl.BlockSpec((None, rt, c), lambda i: (layer, i // hold, 0)),
            pl.BlockSpec((rt, c), lambda i: (i // hold, 0)),
            jax.ShapeDtypeStruct((r, c), BF16))


def _run_casts(srcs, dsts):
    for src, dst in zip(srcs, dsts):
        dst[...] = src[...].astype(BF16)


def _params(*sem):
    return pltpu.CompilerParams(dimension_semantics=sem, vmem_limit_bytes=VMEM_LIMIT_BYTES)


PAIR_W = 2 * LANES
N_PAIRS = N_HEADS // 2
QA_W = N_PAIRS * PAIR_W
C3_ONE = 3 * N_HEADS
N_PIECES = 3


def _split3(c):
    hi = c.astype(BF16).astype(F32)
    r = c - hi
    mid = r.astype(BF16).astype(F32)
    lo = (r - mid).astype(BF16).astype(F32)
    return hi, mid, lo


def _aug_placement():
    e = [[0.0] * D_MODEL for _ in range(LANES)]
    for h in range(N_HEADS):
        base = LANES * (h // 2) + HEAD_DIM * (h % 2)
        for piece in range(N_PIECES):
            e[C3_ONE][base + piece] = 1.0
            e[piece * N_HEADS + h][base + N_PIECES + piece] = 1.0
    return jnp.array(e, BF16)


def _emit_query(x, g, wq_ref, c3, e_ref, qa_ref):
    xn = _rmsnorm(x, g).astype(BF16)
    q = _dot(xn, wq_ref[...]) * (HEAD_DIM ** -0.5)
    aug = _dot(c3, e_ref[...])
    for p in range(N_PAIRS):
        qa_ref[:, PAIR_W * p:PAIR_W * p + LANES] = q[:, LANES * p:LANES * (p + 1)].astype(BF16)
        qa_ref[:, PAIR_W * p + LANES:PAIR_W * (p + 1)] = aug[:, LANES * p:LANES * (p + 1)].astype(BF16)


def _ffn_kernel(*refs, has_attn, has_final, has_query, n_casts):
    it = iter(refs)
    h_ref = next(it)
    if has_attn:
        o_ref, wo_ref = next(it), next(it)
    g_ref, win_ref, wout_ref = next(it), next(it), next(it)
    if has_final:
        gf_ref = next(it)
    if has_query:
        gq_ref, wq_ref, c3_ref, e_ref = next(it), next(it), next(it), next(it)
    cast_srcs = [next(it) for _ in range(n_casts)]
    out_ref = next(it)
    qa_ref = next(it) if has_query else None
    _run_casts(cast_srcs, [next(it) for _ in range(n_casts)])

    x = h_ref[...]
    if has_attn:
        x = x + _dot(o_ref[...], wo_ref[...])
    xn = _rmsnorm(x, g_ref[...]).astype(BF16)
    acc = x
    for c0, cw in FFN_CHUNKS:
        gate = _dot(xn, win_ref[:, c0:c0 + cw])
        up = _dot(xn, win_ref[:, D_FF + c0:D_FF + c0 + cw])
        act = (gate * _sigmoid(gate)) * up
        acc = acc + _dot(act.astype(BF16), wout_ref[c0:c0 + cw, :])
    if has_final:
        acc = _rmsnorm(acc, gf_ref[...])
    out_ref[...] = acc
    if has_query:
        _emit_query(acc, gq_ref[...], wq_ref, c3_ref[...], e_ref, qa_ref)


def _ffn(h2d, layer, norm_ffn, w_in, w_out, attn=None, final_g=None, query=None, casts=()):
    m = h2d.shape[0]
    tile = ROW_TILE
    row = pl.BlockSpec((tile, D_MODEL), lambda i: (i, 0))
    args = [h2d]
    specs = [row]
    if attn is not None:
        o2d, w_o, j = attn
        args += [o2d, w_o]
        specs += [row, _layer_spec((D_MODEL, D_MODEL), j)]
    args += [norm_ffn, w_in[0], w_out[0]]
    specs += [_layer_spec((1, D_MODEL), layer), _weight_spec((D_MODEL, 2 * D_FF), w_in[1]),
              _weight_spec((D_FF, D_MODEL), w_out[1])]
    if final_g is not None:
        args.append(final_g)
        specs.append(_const_spec((1, D_MODEL)))
    out_specs = row
    out_shape = jax.ShapeDtypeStruct((m, D_MODEL), F32)
    if query is not None:
        norm_mix, w_q, j, c3, e = query
        args += [norm_mix, w_q, c3, e]
        specs += [_layer_spec((1, D_MODEL), layer + 1), _layer_spec((D_MODEL, D_MODEL), j),
                  pl.BlockSpec((tile, LANES), lambda i: (i, 0)), _const_spec((LANES, D_MODEL))]
        out_specs = [row, pl.BlockSpec((tile, QA_W), lambda i: (i, 0))]
        out_shape = [out_shape, jax.ShapeDtypeStruct((m, QA_W), BF16)]
    if casts:
        jobs = [_cast_job(w, m // tile, lyr) for w, lyr in casts]
        args += [w for w, _ in casts]
        specs += [j[0] for j in jobs]
        out_specs = ([out_specs] if query is None else out_specs) + [j[1] for j in jobs]
        out_shape = ([out_shape] if query is None else out_shape) + [j[2] for j in jobs]
    return pl.pallas_call(
        functools.partial(_ffn_kernel, has_attn=attn is not None, has_final=final_g is not None,
                          has_query=query is not None, n_casts=len(casts)),
        grid=(m // tile,),
        in_specs=specs,
        out_specs=out_specs,
        out_shape=out_shape,
        compiler_params=_params("parallel"),
        name="ffn",
    )(*args)


REC_STEPS = 64
REC_ROWS = REC_STEPS * SUBLANES
HALO_ROWS = (CONV_W - 1) * SUBLANES


def _rec_kernel(*refs, n_casts, batch_major_in, batch_major_out):
    (h_ref, g_ref, win_ref, cw_ref, cb_ref, wg_ref, bg_ref, lam_ref, wout_ref), refs = refs[:9], refs[9:]
    cast_srcs, out_ref, cast_dsts = refs[:n_casts], refs[n_casts], refs[n_casts + 1:2 * n_casts + 1]
    x_scr, rec_scr, a_scr, u_scr, carry_scr = refs[2 * n_casts + 1:]
    rows = REC_ROWS
    _run_casts(cast_srcs, cast_dsts)

    @pl.when(pl.program_id(0) == 0)
    def _():
        rec_scr[0:HALO_ROWS, :] = jnp.zeros((HALO_ROWS, D_RNN), F32)
        carry_scr[...] = jnp.zeros((SUBLANES, D_RNN), F32)

    if batch_major_in:
        for t in range(REC_STEPS):
            x_scr[t * SUBLANES:(t + 1) * SUBLANES, :] = h_ref[:, t, :]
        x = x_scr[...]
    else:
        x = h_ref[...]
    xn = _rmsnorm(x, g_ref[...]).astype(BF16)
    proj = _dot(xn, win_ref[...])
    rec_scr[HALO_ROWS:HALO_ROWS + rows, :] = proj[:, D_RNN:]

    conv = cb_ref[...]
    for tap in range(CONV_W):
        conv = conv + rec_scr[tap * SUBLANES:tap * SUBLANES + rows, :] * cw_ref[tap:tap + 1, :]
    rec_scr[0:HALO_ROWS, :] = rec_scr[rows:rows + HALO_ROWS, :]

    for n in range(LRU_BLOCKS):
        lo = n * LRU_BLOCK_W
        xb = conv[:, lo:lo + LRU_BLOCK_W]
        half_gates = _dot(xb.astype(BF16), wg_ref[n]) + bg_ref[n]
        gate_i = _sigmoid_of_twice(half_gates[:, :LRU_BLOCK_W])
        gate_r = _sigmoid_of_twice(half_gates[:, LRU_BLOCK_W:])
        log_a = gate_r * (-LRU_C * _softplus(-lam_ref[:, lo:lo + LRU_BLOCK_W]))
        a = jnp.exp(log_a)
        a_scr[:, lo:lo + LRU_BLOCK_W] = a
        one_m_a2 = 1.0 - a * a
        mult = one_m_a2 * lax.rsqrt(jnp.maximum(one_m_a2, TINY))
        u_scr[:, lo:lo + LRU_BLOCK_W] = xb * gate_i * mult

    hs = carry_scr[...]
    for t in range(REC_STEPS):
        r0 = t * SUBLANES
        hs = a_scr[r0:r0 + SUBLANES, :] * hs + u_scr[r0:r0 + SUBLANES, :]
        u_scr[r0:r0 + SUBLANES, :] = hs
    carry_scr[...] = hs
    y = _gelu_tanh(proj[:, :D_RNN]) * u_scr[...]
    res = x + _dot(y.astype(BF16), wout_ref[...])
    if batch_major_out:
        x_scr[...] = res
        for t in range(REC_STEPS):
            out_ref[:, t, :] = x_scr[t * SUBLANES:(t + 1) * SUBLANES, :]
    else:
        out_ref[...] = res


def _rec_layer(h, i, norm_mix, w_in, conv_w, conv_b, w_gates, b_gates, lam, w_out, *, seq_len,
               batch_major_out, casts=()):
    batch_major_in = h.ndim == 3
    b, s = SUBLANES, seq_len
    assert h.size == b * s * D_MODEL, "the (time, batch) row layout needs one batch row per sublane"
    rows = REC_ROWS
    batch_major = pl.BlockSpec((b, REC_STEPS, D_MODEL), lambda si: (0, si, 0))
    time_major = pl.BlockSpec((rows, D_MODEL), lambda si: (si, 0))
    out_shape = (b, s, D_MODEL) if batch_major_out else (s * b, D_MODEL)
    jobs = [_cast_job(w, s // REC_STEPS, lyr) for w, lyr in casts]
    return pl.pallas_call(
        functools.partial(_rec_kernel, n_casts=len(casts), batch_major_in=batch_major_in,
                          batch_major_out=batch_major_out),
        grid=(s // REC_STEPS,),
        in_specs=[
            batch_major if batch_major_in else time_major,
            _layer_spec((1, D_MODEL), i),
            _layer_spec((D_MODEL, 2 * D_RNN), i),
            _layer_spec((CONV_W, D_RNN), i),
            _layer_spec((1, D_RNN), i),
            _layer_spec((LRU_BLOCKS, LRU_BLOCK_W, 2 * LRU_BLOCK_W), i),
            _layer_spec((LRU_BLOCKS, 1, 2 * LRU_BLOCK_W), i),
            _layer_spec((1, D_RNN), i),
            _layer_spec((D_RNN, D_MODEL), i),
        ] + [j[0] for j in jobs],
        out_specs=[batch_major if batch_major_out else time_major] + [j[1] for j in jobs],
        out_shape=[jax.ShapeDtypeStruct(out_shape, F32)] + [j[2] for j in jobs],
        scratch_shapes=[
            pltpu.VMEM((rows, D_MODEL), F32),
            pltpu.VMEM((rows + HALO_ROWS, D_RNN), F32),
            pltpu.VMEM((rows, D_RNN), F32),
            pltpu.VMEM((rows, D_RNN), F32),
            pltpu.VMEM((SUBLANES, D_RNN), F32),
        ],
        compiler_params=_params("arbitrary"),
        name="rec_layer",
    )(h, norm_mix, w_in, conv_w, conv_b, w_gates, b_gates, lam, w_out, *[w for w, _ in casts])


def _kvf_kernel(h_ref, g_ref, wkv_ref, wf_ref, bf_ref, gq_ref, wq_ref, e_ref,
                kt_ref, va_ref, c3_ref, qa_ref, carry_scr):
    ts = SEQ_TILE

    @pl.when(pl.program_id(1) == 0)
    def _():
        carry_scr[...] = jnp.zeros((1, LANES), F32)

    x = h_ref[0]
    xn = _rmsnorm(x, g_ref[...]).astype(BF16)
    kv = _dot(xn, wkv_ref[...])

    f_logit = _dot(xn, wf_ref[...]) + bf_ref[...]
    c = -_softplus(-f_logit)
    row = lax.broadcasted_iota(jnp.int32, (ts, LANES), 0)
    k = 1
    while k < ts:
        c = c + jnp.where(row >= k, pltpu.roll(c, k, 0), 0.0)
        k *= 2
    c = c + carry_scr[...]
    carry_scr[...] = c[ts - 1:ts, :]

    lane = lax.broadcasted_iota(jnp.int32, (ts, LANES), 1)
    hi, mid, lo = _split3(c)
    c3 = jnp.where(lane < N_HEADS, hi,
                   jnp.where(lane < 2 * N_HEADS, pltpu.roll(mid, N_HEADS, 1),
                             jnp.where(lane < C3_ONE, pltpu.roll(lo, 2 * N_HEADS, 1),
                                       jnp.where(lane == C3_ONE, 1.0, 0.0)))).astype(BF16)
    c3_ref[0] = c3
    _emit_query(x, gq_ref[...], wq_ref, c3, e_ref, qa_ref.at[0])

    c_t = c.T
    ones_tile = jnp.where(lane == 0, 1.0, 0.0).astype(BF16)
    trow = lax.broadcasted_iota(jnp.int32, (LANES, ts), 0)
    arow = lax.broadcasted_iota(jnp.int32, (BF16_ROWS, ts), 0)
    for p in range(N_PAIRS):
        va_ref[0, p, :, 0:LANES] = kv[:, D_MODEL + LANES * p:D_MODEL + LANES * (p + 1)].astype(BF16)
        va_ref[0, p, :, LANES:PAIR_W] = ones_tile
        k_t = kv[:, LANES * p:LANES * (p + 1)].T
        for j in range(2):
            h = 2 * p + j
            own = (trow >= HEAD_DIM * j) & (trow < HEAD_DIM * (j + 1))
            kt_ref[0, h, 0:LANES, :] = jnp.where(own, k_t, 0.0).astype(BF16)
            chi, cmid, clo = _split3(c_t[h:h + 1, :])
            aug = jnp.where(arow == 0, -chi, jnp.where(arow == 1, -cmid, jnp.where(arow == 2, -clo,
                            jnp.where(arow < 2 * N_PIECES, 1.0, 0.0))))
            kt_ref[0, h, LANES:PAIR_W, :] = jnp.zeros((LANES, ts), BF16)
            a0 = LANES + HEAD_DIM * j
            kt_ref[0, h, a0:a0 + BF16_ROWS, :] = aug.astype(BF16)


def _kvf(h, norm_kv, w_kv, w_f, b_f, norm_mix, w_q, e):
    b, s, _ = h.shape
    ts = SEQ_TILE
    return pl.pallas_call(
        _kvf_kernel,
        grid=(b, s // ts),
        in_specs=[
            pl.BlockSpec((1, ts, D_MODEL), lambda bi, si: (bi, si, 0)),
            _const_spec((1, D_MODEL)),
            _const_spec((D_MODEL, 2 * D_MODEL)),
            _const_spec((D_MODEL, LANES)),
            _const_spec((1, LANES)),
            _layer_spec((1, D_MODEL), N_A),
            _layer_spec((D_MODEL, D_MODEL), 0),
            _const_spec((LANES, D_MODEL)),
        ],
        out_specs=[
            pl.BlockSpec((1, N_HEADS, PAIR_W, ts), lambda bi, si: (bi, 0, 0, si)),
            pl.BlockSpec((1, N_PAIRS, ts, PAIR_W), lambda bi, si: (bi, 0, si, 0)),
            pl.BlockSpec((1, ts, LANES), lambda bi, si: (bi, si, 0)),
            pl.BlockSpec((1, ts, QA_W), lambda bi, si: (bi, si, 0)),
        ],
        out_shape=[
            jax.ShapeDtypeStruct((b, N_HEADS, PAIR_W, s), BF16),
            jax.ShapeDtypeStruct((b, N_PAIRS, s, PAIR_W), BF16),
            jax.ShapeDtypeStruct((b, s, LANES), BF16),
            jax.ShapeDtypeStruct((b, s, QA_W), BF16),
        ],
        scratch_shapes=[pltpu.VMEM((1, LANES), F32)],
        compiler_params=_params("parallel", "arbitrary"),
        name="kvf",
    )(h, norm_kv, w_kv, w_f, b_f, norm_mix, w_q, e)


def _attn_kernel(qa_ref, kt_ref, va_ref, o_ref):
    t = ATT_TILE
    s_len = qa_ref.shape[1]
    tri = (lax.broadcasted_iota(jnp.int32, (t, t), 0) >= lax.broadcasted_iota(jnp.int32, (t, t), 1))
    lane = lax.broadcasted_iota(jnp.int32, (t, LANES), 1)

    def logits(i, j):
        lo = i * t
        s = _dot(qa_ref[0, lo:lo + t, :], kt_ref[0, j, :, 0:lo + t])
        s_diag = jnp.where(tri, s[:, lo:], -jnp.inf)
        s = jnp.concatenate([s[:, :lo], s_diag], axis=1) if i > 0 else s_diag
        return s, jnp.max(s, axis=-1, keepdims=True)

    def weighted_values(i, s, m):
        acc = _dot(jnp.exp(s - m).astype(BF16), va_ref[0, 0, 0:(i + 1) * t, :])
        return acc[:, :LANES] * (1.0 / acc[:, LANES:LANES + 1])

    units = [(i, j) for i in range(s_len // t) for j in range(2)]
    pending = logits(*units[0])
    even = None
    for n, (i, j) in enumerate(units):
        ahead = logits(*units[n + 1]) if n + 1 < len(units) else None
        out = weighted_values(i, *pending)
        pending = ahead
        if j == 0:
            even = out
        else:
            o_ref[0, i * t:(i + 1) * t, :] = jnp.where(lane < HEAD_DIM, even, out).astype(BF16)


def _attention(qa, kt, va):
    b, s, _ = qa.shape
    return pl.pallas_call(
        _attn_kernel,
        grid=(b, N_PAIRS),
        in_specs=[
            pl.BlockSpec((1, s, PAIR_W), lambda bi, p: (bi, 0, p)),
            pl.BlockSpec((1, 2, PAIR_W, s), lambda bi, p: (bi, p, 0, 0)),
            pl.BlockSpec((1, 1, s, PAIR_W), lambda bi, p: (bi, p, 0, 0)),
        ],
        out_specs=pl.BlockSpec((1, s, LANES), lambda bi, p: (bi, 0, p)),
        out_shape=jax.ShapeDtypeStruct((b, s, D_MODEL), BF16),
        compiler_params=_params("parallel", "parallel"),
        name="fox_attention",
    )(qa, kt, va)


def kernel(x, norm_mix, norm_ffn, w_ffn_in, w_ffn_out, w_rec_in, conv_w, conv_b, w_lru_gates,
           b_lru_gates, lru_param, w_rec_out, norm_kv, w_kvf, b_forget, w_q, w_o, norm_final):
    b, s, d = x.shape
    m = b * s
    row = lambda a: a.reshape(1, -1)
    rows = lambda a: a.reshape(a.shape[0], 1, a.shape[1])

    norm_mix, norm_ffn = rows(norm_mix), rows(norm_ffn)
    w_rec_in, w_rec_out = w_rec_in.astype(BF16), w_rec_out.astype(BF16)
    w_lru_gates = (0.5 * w_lru_gates).astype(BF16)
    b_lru_gates = (0.5 * b_lru_gates).reshape(N_A, LRU_BLOCKS, 1, 2 * LRU_BLOCK_W)
    conv_b, lru_param = rows(conv_b), rows(lru_param)
    w_q, w_o = w_q.astype(BF16), w_o.astype(BF16)

    h = x
    for i in range(N_A):
        last = i == N_A - 1
        rec = _rec_layer(h, i, norm_mix, w_rec_in, conv_w, conv_b, w_lru_gates, b_lru_gates, lru_param, w_rec_out,
                         seq_len=s, batch_major_out=last, casts=((w_ffn_in, 0), (w_ffn_out, 0)) if i == 0 else ())
        if i == 0:
            h, fin0, fout0 = rec
            h, w_ffn_in, w_ffn_out = _ffn(h.reshape(m, d), i, norm_ffn, (fin0, None), (fout0, None),
                                          casts=((w_ffn_in, None), (w_ffn_out, None)))
        else:
            h = _ffn(rec[0].reshape(m, d), i, norm_ffn, (w_ffn_in, i), (w_ffn_out, i))
    h = h.reshape(b, s, d)

    w_kv = w_kvf[:, :2 * D_MODEL].astype(BF16)
    w_f = jnp.pad(w_kvf[:, 2 * D_MODEL:], ((0, 0), (0, LANES - N_HEADS))).astype(BF16)
    b_f = jnp.pad(b_forget, (0, LANES - N_HEADS)).reshape(1, LANES)
    e = _aug_placement()
    kt, va, c3, qa = _kvf(h, row(norm_kv), w_kv, w_f, b_f, norm_mix, w_q, e)
    c3 = c3.reshape(m, LANES)

    h2d = h.reshape(m, d)
    for j in range(DEPTH - N_A):
        layer = N_A + j
        o = _attention(qa, kt, va)
        attn = (o.reshape(m, d), w_o, j)
        if layer == DEPTH - 1:
            h2d = _ffn(h2d, layer, norm_ffn, (w_ffn_in, layer), (w_ffn_out, layer), attn=attn,
                       final_g=row(norm_final))
        else:
            h2d, qa = _ffn(h2d, layer, norm_ffn, (w_ffn_in, layer), (w_ffn_out, layer), attn=attn,
                           query=(norm_mix, w_q, j + 1, c3, e))
            qa = qa.reshape(b, s, QA_W)
    return h2d.reshape(b, s, d)
```

```python
import functools
import math

import jax
import jax.numpy as jnp
from jax import lax
from jax.experimental import pallas as pl
from jax.experimental.pallas import tpu as pltpu

D_MODEL = 1024
DEPTH = 4
N_A = DEPTH // 2
D_RNN = D_MODEL
LRU_BLOCK_W = 256
LRU_BLOCKS = D_RNN // LRU_BLOCK_W
CONV_W = 4
LRU_C = 8.0
N_HEADS = 16
HEAD_DIM = D_MODEL // N_HEADS
D_FF = 2816
EPS = 1e-6

SUBLANES = 8
LANES = 128
BF16_ROWS = 2 * SUBLANES
VMEM_LIMIT_BYTES = 56 * 1024 * 1024
TINY = 1e-30

F32 = jnp.float32
BF16 = jnp.bfloat16

ROW_TILE = 512
SEQ_TILE = 512
ATT_TILE = 256
FFN_CHUNKS = ((0, 512), (512, 512), (1024, 512), (1536, 512), (2048, 512), (2560, 256))


def _rmsnorm(x, g):
    y = x * lax.rsqrt(jnp.mean(x * x, axis=-1, keepdims=True) + EPS)
    return y * g


def _sigmoid(x):
    return 1.0 / (1.0 + jnp.exp(-x))


def _softplus(x):
    return jnp.maximum(x, 0.0) + jnp.log1p(jnp.exp(-jnp.abs(x)))


def _sigmoid_of_twice(half_x):
    return 0.5 * jnp.tanh(half_x) + 0.5


def _gelu_tanh(x):
    c = math.sqrt(2.0 / math.pi)
    half = 0.5 * x
    return half + half * jnp.tanh(x * (c + (c * 0.044715) * (x * x)))


def _dot(a, b):
    return jnp.dot(a, b, preferred_element_type=F32)


def _const_spec(shape):
    n = len(shape)
    return pl.BlockSpec(shape, lambda *_: (0,) * n, pipeline_mode=pl.Buffered(1))


def _layer_spec(shape, layer):
    n = len(shape)
    return pl.BlockSpec((None,) + tuple(shape), lambda *_: (layer,) + (0,) * n, pipeline_mode=pl.Buffered(1))


def _weight_spec(shape, layer):
    return _const_spec(shape) if layer is None else _layer_spec(shape, layer)


def _cast_job(w, steps, layer):
    n_layers, r, c = w.shape
    hold = next(k for k in (1, 2, 4, 8) if r % (steps // k) == 0 and (r // (steps // k)) % BF16_ROWS == 0)
    rt = r // (steps // hold)
    if layer is None:
        return (pl.BlockSpec((n_layers, rt, c), lambda i: (0, i // hold, 0)),
                pl.BlockSpec((n_layers, rt, c), lambda i: (0, i // hold, 0)),
                jax.ShapeDtypeStruct(w.shape, BF16))
    return (pl.BlockSpec((None, rt, c), lambda i: (layer, i // hold, 0)),
            pl.BlockSpec((rt, c), lambda i: (i // hold, 0)),
            jax.ShapeDtypeStruct((r, c), BF16))


def _run_casts(srcs, dsts):
    for src, dst in zip(srcs, dsts):
        dst[...] = src[...].astype(BF16)


def _params(*sem):
    return pltpu.CompilerParams(dimension_semantics=sem, vmem_limit_bytes=VMEM_LIMIT_BYTES)


PAIR_W = 2 * LANES
N_PAIRS = N_HEADS // 2
QA_W = N_PAIRS * PAIR_W
C3_ONE = 3 * N_HEADS
N_PIECES = 3


def _split3(c):
    hi = c.astype(BF16).astype(F32)
    r = c - hi
    mid = r.astype(BF16).astype(F32)
    lo = (r - mid).astype(BF16).astype(F32)
    return hi, mid, lo


def _aug_placement():
    e = [[0.0] * D_MODEL for _ in range(LANES)]
    for h in range(N_HEADS):
        base = LANES * (h // 2) + HEAD_DIM * (h % 2)
        for piece in range(N_PIECES):
            e[C3_ONE][base + piece] = 1.0
            e[piece * N_HEADS + h][base + N_PIECES + piece] = 1.0
    return jnp.array(e, BF16)


def _query_proj(x, g, wq_ref):
    xn = _rmsnorm(x, g).astype(BF16)
    return _dot(xn, wq_ref[...]) * (HEAD_DIM ** -0.5)


def _store_query(q, c3, e_ref, qa_ref):
    aug = _dot(c3, e_ref[...])
    for p in range(N_PAIRS):
        qa_ref[:, PAIR_W * p:PAIR_W * p + LANES] = q[:, LANES * p:LANES * (p + 1)].astype(BF16)
        qa_ref[:, PAIR_W * p + LANES:PAIR_W * (p + 1)] = aug[:, LANES * p:LANES * (p + 1)].astype(BF16)


def _ffn_kernel(*refs, has_attn, has_final, has_query, n_casts):
    it = iter(refs)
    h_ref = next(it)
    if has_attn:
        o_ref, wo_ref = next(it), next(it)
    g_ref, win_ref, wout_ref = next(it), next(it), next(it)
    if has_final:
        gf_ref = next(it)
    if has_query:
        gq_ref, wq_ref, c3_ref, e_ref = next(it), next(it), next(it), next(it)
    cast_srcs = [next(it) for _ in range(n_casts)]
    out_ref = next(it)
    qa_ref = next(it) if has_query else None
    _run_casts(cast_srcs, [next(it) for _ in range(n_casts)])

    x = h_ref[...]
    if has_attn:
        x = x + _dot(o_ref[...], wo_ref[...])
    xn = _rmsnorm(x, g_ref[...]).astype(BF16)
    acc = x
    for c0, cw in FFN_CHUNKS:
        gate = _dot(xn, win_ref[:, c0:c0 + cw])
        up = _dot(xn, win_ref[:, D_FF + c0:D_FF + c0 + cw])
        act = (gate * _sigmoid(gate)) * up
        acc = acc + _dot(act.astype(BF16), wout_ref[c0:c0 + cw, :])
    if has_final:
        acc = _rmsnorm(acc, gf_ref[...])
    out_ref[...] = acc
    if has_query:
        _store_query(_query_proj(acc, gq_ref[...], wq_ref), c3_ref[...], e_ref, qa_ref)


def _ffn(h2d, layer, norm_ffn, w_in, w_out, attn=None, final_g=None, query=None, casts=()):
    m = h2d.shape[0]
    tile = ROW_TILE
    row = pl.BlockSpec((tile, D_MODEL), lambda i: (i, 0))
    args = [h2d]
    specs = [row]
    if attn is not None:
        o2d, w_o, j = attn
        args += [o2d, w_o]
        specs += [row, _layer_spec((D_MODEL, D_MODEL), j)]
    args += [norm_ffn, w_in[0], w_out[0]]
    specs += [_layer_spec((1, D_MODEL), layer), _weight_spec((D_MODEL, 2 * D_FF), w_in[1]),
              _weight_spec((D_FF, D_MODEL), w_out[1])]
    if final_g is not None:
        args.append(final_g)
        specs.append(_const_spec((1, D_MODEL)))
    out_specs = row
    out_shape = jax.ShapeDtypeStruct((m, D_MODEL), F32)
    if query is not None:
        norm_mix, w_q, j, c3, e = query
        args += [norm_mix, w_q, c3, e]
        specs += [_layer_spec((1, D_MODEL), layer + 1), _layer_spec((D_MODEL, D_MODEL), j),
                  pl.BlockSpec((tile, LANES), lambda i: (i, 0)), _const_spec((LANES, D_MODEL))]
        out_specs = [row, pl.BlockSpec((tile, QA_W), lambda i: (i, 0))]
        out_shape = [out_shape, jax.ShapeDtypeStruct((m, QA_W), BF16)]
    if casts:
        jobs = [_cast_job(w, m // tile, lyr) for w, lyr in casts]
        args += [w for w, _ in casts]
        specs += [j[0] for j in jobs]
        out_specs = ([out_specs] if query is None else out_specs) + [j[1] for j in jobs]
        out_shape = ([out_shape] if query is None else out_shape) + [j[2] for j in jobs]
    return pl.pallas_call(
        functools.partial(_ffn_kernel, has_attn=attn is not None, has_final=final_g is not None,
                          has_query=query is not None, n_casts=len(casts)),
        grid=(m // tile,),
        in_specs=specs,
        out_specs=out_specs,
        out_shape=out_shape,
        compiler_params=_params("parallel"),
        name="ffn",
    )(*args)


REC_STEPS = 64
REC_ROWS = REC_STEPS * SUBLANES
HALO_ROWS = (CONV_W - 1) * SUBLANES


def _rec_kernel(*refs, n_casts, batch_major_in, batch_major_out):
    (h_ref, g_ref, win_ref, cw_ref, cb_ref, wg_ref, bg_ref, lam_ref, wout_ref), refs = refs[:9], refs[9:]
    cast_srcs, out_ref, cast_dsts = refs[:n_casts], refs[n_casts], refs[n_casts + 1:2 * n_casts + 1]
    x_scr, rec_scr, a_scr, u_scr, carry_scr = refs[2 * n_casts + 1:]
    rows = REC_ROWS
    _run_casts(cast_srcs, cast_dsts)

    @pl.when(pl.program_id(0) == 0)
    def _():
        rec_scr[0:HALO_ROWS, :] = jnp.zeros((HALO_ROWS, D_RNN), F32)
        carry_scr[...] = jnp.zeros((SUBLANES, D_RNN), F32)

    if batch_major_in:
        for t in range(REC_STEPS):
            x_scr[t * SUBLANES:(t + 1) * SUBLANES, :] = h_ref[:, t, :]
        x = x_scr[...]
    else:
        x = h_ref[...]
    xn = _rmsnorm(x, g_ref[...]).astype(BF16)
    proj = _dot(xn, win_ref[...])
    rec_scr[HALO_ROWS:HALO_ROWS + rows, :] = proj[:, D_RNN:]

    conv = cb_ref[...]
    for tap in range(CONV_W):
        conv = conv + rec_scr[tap * SUBLANES:tap * SUBLANES + rows, :] * cw_ref[tap:tap + 1, :]
    rec_scr[0:HALO_ROWS, :] = rec_scr[rows:rows + HALO_ROWS, :]

    for n in range(LRU_BLOCKS):
        lo = n * LRU_BLOCK_W
        xb = conv[:, lo:lo + LRU_BLOCK_W]
        half_gates = _dot(xb.astype(BF16), wg_ref[n]) + bg_ref[n]
        gate_i = _sigmoid_of_twice(half_gates[:, :LRU_BLOCK_W])
        gate_r = _sigmoid_of_twice(half_gates[:, LRU_BLOCK_W:])
        log_a = gate_r * (-LRU_C * _softplus(-lam_ref[:, lo:lo + LRU_BLOCK_W]))
        a = jnp.exp(log_a)
        a_scr[:, lo:lo + LRU_BLOCK_W] = a
        one_m_a2 = 1.0 - a * a
        mult = one_m_a2 * lax.rsqrt(jnp.maximum(one_m_a2, TINY))
        u_scr[:, lo:lo + LRU_BLOCK_W] = xb * gate_i * mult

    hs = carry_scr[...]
    for t in range(REC_STEPS):
        r0 = t * SUBLANES
        hs = a_scr[r0:r0 + SUBLANES, :] * hs + u_scr[r0:r0 + SUBLANES, :]
        u_scr[r0:r0 + SUBLANES, :] = hs
    carry_scr[...] = hs
    y = _gelu_tanh(proj[:, :D_RNN]) * u_scr[...]
    res = x + _dot(y.astype(BF16), wout_ref[...])
    if batch_major_out:
        x_scr[...] = res
        for t in range(REC_STEPS):
            out_ref[:, t, :] = x_scr[t * SUBLANES:(t + 1) * SUBLANES, :]
    else:
        out_ref[...] = res


def _rec_layer(h, i, norm_mix, w_in, conv_w, conv_b, w_gates, b_gates, lam, w_out, *, seq_len,
               batch_major_out, casts=()):
    batch_major_in = h.ndim == 3
    b, s = SUBLANES, seq_len
    assert h.size == b * s * D_MODEL, "the (time, batch) row layout needs one batch row per sublane"
    rows = REC_ROWS
    batch_major = pl.BlockSpec((b, REC_STEPS, D_MODEL), lambda si: (0, si, 0))
    time_major = pl.BlockSpec((rows, D_MODEL), lambda si: (si, 0))
    out_shape = (b, s, D_MODEL) if batch_major_out else (s * b, D_MODEL)
    jobs = [_cast_job(w, s // REC_STEPS, lyr) for w, lyr in casts]
    return pl.pallas_call(
        functools.partial(_rec_kernel, n_casts=len(casts), batch_major_in=batch_major_in,
                          batch_major_out=batch_major_out),
        grid=(s // REC_STEPS,),
        in_specs=[
            batch_major if batch_major_in else time_major,
            _layer_spec((1, D_MODEL), i),
            _layer_spec((D_MODEL, 2 * D_RNN), i),
            _layer_spec((CONV_W, D_RNN), i),
            _layer_spec((1, D_RNN), i),
            _layer_spec((LRU_BLOCKS, LRU_BLOCK_W, 2 * LRU_BLOCK_W), i),
            _layer_spec((LRU_BLOCKS, 1, 2 * LRU_BLOCK_W), i),
            _layer_spec((1, D_RNN), i),
            _layer_spec((D_RNN, D_MODEL), i),
        ] + [j[0] for j in jobs],
        out_specs=[batch_major if batch_major_out else time_major] + [j[1] for j in jobs],
        out_shape=[jax.ShapeDtypeStruct(out_shape, F32)] + [j[2] for j in jobs],
        scratch_shapes=[
            pltpu.VMEM((rows, D_MODEL), F32),
            pltpu.VMEM((rows + HALO_ROWS, D_RNN), F32),
            pltpu.VMEM((rows, D_RNN), F32),
            pltpu.VMEM((rows, D_RNN), F32),
            pltpu.VMEM((SUBLANES, D_RNN), F32),
        ],
        compiler_params=_params("arbitrary"),
        name="rec_layer",
    )(h, norm_mix, w_in, conv_w, conv_b, w_gates, b_gates, lam, w_out, *[w for w, _ in casts])


def _kvf_kernel(h_ref, g_ref, wkv_ref, wf_ref, bf_ref, gq_ref, wq_ref, e_ref,
                kt_ref, va_ref, c3_ref, qa_ref, carry_scr):
    ts = SEQ_TILE

    @pl.when(pl.program_id(1) == 0)
    def _():
        carry_scr[...] = jnp.zeros((1, LANES), F32)

    x = h_ref[0]
    xn = _rmsnorm(x, g_ref[...]).astype(BF16)
    kv = _dot(xn, wkv_ref[...])

    f_logit = _dot(xn, wf_ref[...]) + bf_ref[...]
    c = -_softplus(-f_logit)
    row = lax.broadcasted_iota(jnp.int32, (ts, LANES), 0)
    k = 1
    while k < ts:
        c = c + jnp.where(row >= k, pltpu.roll(c, k, 0), 0.0)
        k *= 2
    c = c + carry_scr[...]
    carry_scr[...] = c[ts - 1:ts, :]

    lane = lax.broadcasted_iota(jnp.int32, (ts, LANES), 1)
    hi, mid, lo = _split3(c)
    c3 = jnp.where(lane < N_HEADS, hi,
                   jnp.where(lane < 2 * N_HEADS, pltpu.roll(mid, N_HEADS, 1),
                             jnp.where(lane < C3_ONE, pltpu.roll(lo, 2 * N_HEADS, 1),
                                       jnp.where(lane == C3_ONE, 1.0, 0.0)))).astype(BF16)
    c3_ref[0] = c3
    _store_query(_query_proj(x, gq_ref[...], wq_ref), c3, e_ref, qa_ref.at[0])

    c_t = c.T
    ones_tile = jnp.where(lane == 0, 1.0, 0.0).astype(BF16)
    trow = lax.broadcasted_iota(jnp.int32, (LANES, ts), 0)
    arow = lax.broadcasted_iota(jnp.int32, (BF16_ROWS, ts), 0)
    for p in range(N_PAIRS):
        va_ref[0, p, :, 0:LANES] = kv[:, D_MODEL + LANES * p:D_MODEL + LANES * (p + 1)].astype(BF16)
        va_ref[0, p, :, LANES:PAIR_W] = ones_tile
        k_t = kv[:, LANES * p:LANES * (p + 1)].T
        for j in range(2):
            h = 2 * p + j
            own = (trow >= HEAD_DIM * j) & (trow < HEAD_DIM * (j + 1))
            kt_ref[0, h, 0:LANES, :] = jnp.where(own, k_t, 0.0).astype(BF16)
            chi, cmid, clo = _split3(c_t[h:h + 1, :])
            aug = jnp.where(arow == 0, -chi, jnp.where(arow == 1, -cmid, jnp.where(arow == 2, -clo,
                            jnp.where(arow < 2 * N_PIECES, 1.0, 0.0))))
            kt_ref[0, h, LANES:PAIR_W, :] = jnp.zeros((LANES, ts), BF16)
            a0 = LANES + HEAD_DIM * j
            kt_ref[0, h, a0:a0 + BF16_ROWS, :] = aug.astype(BF16)


def _kvf(h, norm_kv, w_kv, w_f, b_f, norm_mix, w_q, e):
    b, s, _ = h.shape
    ts = SEQ_TILE
    return pl.pallas_call(
        _kvf_kernel,
        grid=(b, s // ts),
        in_specs=[
            pl.BlockSpec((1, ts, D_MODEL), lambda bi, si: (bi, si, 0)),
            _const_spec((1, D_MODEL)),
            _layer_spec((D_MODEL, 2 * D_MODEL), 0),
            _const_spec((D_MODEL, LANES)),
            _const_spec((1, LANES)),
            _layer_spec((1, D_MODEL), N_A),
            _layer_spec((D_MODEL, D_MODEL), 0),
            _const_spec((LANES, D_MODEL)),
        ],
        out_specs=[
            pl.BlockSpec((1, N_HEADS, PAIR_W, ts), lambda bi, si: (bi, 0, 0, si)),
            pl.BlockSpec((1, N_PAIRS, ts, PAIR_W), lambda bi, si: (bi, 0, si, 0)),
            pl.BlockSpec((1, ts, LANES), lambda bi, si: (bi, si, 0)),
            pl.BlockSpec((1, ts, QA_W), lambda bi, si: (bi, si, 0)),
        ],
        out_shape=[
            jax.ShapeDtypeStruct((b, N_HEADS, PAIR_W, s), BF16),
            jax.ShapeDtypeStruct((b, N_PAIRS, s, PAIR_W), BF16),
            jax.ShapeDtypeStruct((b, s, LANES), BF16),
            jax.ShapeDtypeStruct((b, s, QA_W), BF16),
        ],
        scratch_shapes=[pltpu.VMEM((1, LANES), F32)],
        compiler_params=_params("parallel", "arbitrary"),
        name="kvf",
    )(h, norm_kv, w_kv, w_f, b_f, norm_mix, w_q, e)


ATT_PAIRS = 2


def _attn_kernel(qa_ref, kt_ref, va_ref, o_ref):
    t = ATT_TILE
    s_len = qa_ref.shape[1]
    tri = (lax.broadcasted_iota(jnp.int32, (t, t), 0) >= lax.broadcasted_iota(jnp.int32, (t, t), 1))
    lane = lax.broadcasted_iota(jnp.int32, (t, LANES), 1)

    def logits(i, j):
        lo = i * t
        p = j // 2
        s = _dot(qa_ref[0, lo:lo + t, PAIR_W * p:PAIR_W * (p + 1)], kt_ref[0, j, :, 0:lo + t])
        s_diag = jnp.where(tri, s[:, lo:], -jnp.inf)
        s = jnp.concatenate([s[:, :lo], s_diag], axis=1) if i > 0 else s_diag
        return s, jnp.max(s, axis=-1, keepdims=True)

    def weighted_values(i, j, s, m):
        acc = _dot(jnp.exp(s - m).astype(BF16), va_ref[0, j // 2, 0:(i + 1) * t, :])
        return acc[:, :LANES] * (1.0 / acc[:, LANES:LANES + 1])

    units = [(i, j) for i in range(s_len // t) for j in range(2 * ATT_PAIRS)]
    pending = logits(*units[0])
    even = None
    for n, (i, j) in enumerate(units):
        ahead = logits(*units[n + 1]) if n + 1 < len(units) else None
        out = weighted_values(i, j, *pending)
        pending = ahead
        if j % 2 == 0:
            even = out
        else:
            p = j // 2
            o_ref[0, i * t:(i + 1) * t, LANES * p:LANES * (p + 1)] = (
                jnp.where(lane < HEAD_DIM, even, out).astype(BF16))


def _attention(qa, kt, va):
    b, s, _ = qa.shape
    return pl.pallas_call(
        _attn_kernel,
        grid=(b, N_PAIRS // ATT_PAIRS),
        in_specs=[
            pl.BlockSpec((1, s, PAIR_W * ATT_PAIRS), lambda bi, p: (bi, 0, p)),
            pl.BlockSpec((1, 2 * ATT_PAIRS, PAIR_W, s), lambda bi, p: (bi, p, 0, 0)),
            pl.BlockSpec((1, ATT_PAIRS, s, PAIR_W), lambda bi, p: (bi, p, 0, 0)),
        ],
        out_specs=pl.BlockSpec((1, s, LANES * ATT_PAIRS), lambda bi, p: (bi, 0, p)),
        out_shape=jax.ShapeDtypeStruct((b, s, D_MODEL), BF16),
        compiler_params=_params("parallel", "parallel"),
        name="fox_attention",
    )(qa, kt, va)


def kernel(x, norm_mix, norm_ffn, w_ffn_in, w_ffn_out, w_rec_in, conv_w, conv_b, w_lru_gates,
           b_lru_gates, lru_param, w_rec_out, norm_kv, w_kvf, b_forget, w_q, w_o, norm_final):
    b, s, d = x.shape
    m = b * s
    row = lambda a: a.reshape(1, -1)
    rows = lambda a: a.reshape(a.shape[0], 1, a.shape[1])

    norm_mix, norm_ffn = rows(norm_mix), rows(norm_ffn)
    w_rec_in, w_rec_out = w_rec_in.astype(BF16), w_rec_out.astype(BF16)
    w_lru_gates = (0.5 * w_lru_gates).astype(BF16)
    b_lru_gates = (0.5 * b_lru_gates).reshape(N_A, LRU_BLOCKS, 1, 2 * LRU_BLOCK_W)
    conv_b, lru_param = rows(conv_b), rows(lru_param)

    h = x
    for i in range(N_A):
        last = i == N_A - 1
        rec = _rec_layer(h, i, norm_mix, w_rec_in, conv_w, conv_b, w_lru_gates, b_lru_gates, lru_param, w_rec_out,
                         seq_len=s, batch_major_out=last, casts=((w_ffn_in, 0), (w_ffn_out, 0)) if i == 0 else ())
        if i == 0:
            h, fin0, fout0 = rec
            h, w_ffn_in, w_ffn_out = _ffn(h.reshape(m, d), i, norm_ffn, (fin0, None), (fout0, None),
                                          casts=((w_ffn_in, None), (w_ffn_out, None)))
        elif last:
            h, w_kvf, w_q, w_o = _ffn(rec[0].reshape(m, d), i, norm_ffn, (w_ffn_in, i), (w_ffn_out, i),
                                      casts=((w_kvf[None], None), (w_q, None), (w_o, None)))
        else:
            h = _ffn(rec[0].reshape(m, d), i, norm_ffn, (w_ffn_in, i), (w_ffn_out, i))
    h = h.reshape(b, s, d)

    w_f = jnp.pad(w_kvf[0, :, 2 * D_MODEL:], ((0, 0), (0, LANES - N_HEADS)))
    b_f = jnp.pad(b_forget, (0, LANES - N_HEADS)).reshape(1, LANES)
    e = _aug_placement()
    kt, va, c3, qa = _kvf(h, row(norm_kv), w_kvf, w_f, b_f, norm_mix, w_q, e)
    c3 = c3.reshape(m, LANES)

    h2d = h.reshape(m, d)
    for j in range(DEPTH - N_A):
        layer = N_A + j
        o = _attention(qa, kt, va)
        attn = (o.reshape(m, d), w_o, j)
        if layer == DEPTH - 1:
            h2d = _ffn(h2d, layer, norm_ffn, (w_ffn_in, layer), (w_ffn_out, layer), attn=attn,
                       final_g=row(norm_final))
        else:
            h2d, qa = _ffn(h2d, layer, norm_ffn, (w_ffn_in, layer), (w_ffn_out, layer), attn=attn,
                           query=(norm_mix, w_q, j + 1, c3, e))
            qa = qa.reshape(b, s, QA_W)
    return h2d.reshape(b, s, d)
```

```python
import functools
import math

import jax
import jax.numpy as jnp
from jax import lax
from jax.experimental import pallas as pl
from jax.experimental.pallas import tpu as pltpu

D_MODEL = 1024
DEPTH = 4
N_A = DEPTH // 2
D_RNN = D_MODEL
LRU_BLOCK_W = 256
LRU_BLOCKS = D_RNN // LRU_BLOCK_W
CONV_W = 4
LRU_C = 8.0
N_HEADS = 16
HEAD_DIM = D_MODEL // N_HEADS
D_FF = 2816
EPS = 1e-6

SUBLANES = 8
LANES = 128
BF16_ROWS = 2 * SUBLANES
VMEM_LIMIT_BYTES = 56 * 1024 * 1024
TINY = 1e-30

F32 = jnp.float32
BF16 = jnp.bfloat16

ROW_TILE = 512
SEQ_TILE = 512
ATT_TILE = 256
FFN_CHUNKS = ((0, 512), (512, 512), (1024, 512), (1536, 512), (2048, 512), (2560, 256))


def _rmsnorm(x, g):
    y = x * lax.rsqrt(jnp.mean(x * x, axis=-1, keepdims=True) + EPS)
    return y * g


def _sigmoid(x):
    return 1.0 / (1.0 + jnp.exp(-x))


def _softplus(x):
    return jnp.maximum(x, 0.0) + jnp.log1p(jnp.exp(-jnp.abs(x)))


def _sigmoid_of_twice(half_x):
    return 0.5 * jnp.tanh(half_x) + 0.5


def _gelu_tanh(x):
    c = math.sqrt(2.0 / math.pi)
    half = 0.5 * x
    return half + half * jnp.tanh(x * (c + (c * 0.044715) * (x * x)))


def _dot(a, b):
    return jnp.dot(a, b, preferred_element_type=F32)


def _const_spec(shape):
    n = len(shape)
    return pl.BlockSpec(shape, lambda *_: (0,) * n, pipeline_mode=pl.Buffered(1))


def _layer_spec(shape, layer):
    n = len(shape)
    return pl.BlockSpec((None,) + tuple(shape), lambda *_: (layer,) + (0,) * n, pipeline_mode=pl.Buffered(1))


def _weight_spec(shape, layer):
    return _const_spec(shape) if layer is None else _layer_spec(shape, layer)


def _cast_job(w, steps, layer):
    r, c = w.shape[-2:]
    hold = next(k for k in (1, 2, 4, 8) if r % (steps // k) == 0 and (r // (steps // k)) % BF16_ROWS == 0)
    rt = r // (steps // hold)
    if w.ndim == 2:
        spec = pl.BlockSpec((rt, c), lambda i: (i // hold, 0))
        return spec, spec, jax.ShapeDtypeStruct(w.shape, BF16)
    n_layers = w.shape[0]
    if layer is None:
        return (pl.BlockSpec((n_layers, rt, c), lambda i: (0, i // hold, 0)),
                pl.BlockSpec((n_layers, rt, c), lambda i: (0, i // hold, 0)),
                jax.ShapeDtypeStruct(w.shape, BF16))
    return (pl.BlockSpec((None, rt, c), lambda i: (layer, i // hold, 0)),
            pl.BlockSpec((rt, c), lambda i: (i // hold, 0)),
            jax.ShapeDtypeStruct((r, c), BF16))


def _run_casts(srcs, dsts):
    for src, dst in zip(srcs, dsts):
        dst[...] = src[...].astype(BF16)


def _params(*sem):
    return pltpu.CompilerParams(dimension_semantics=sem, vmem_limit_bytes=VMEM_LIMIT_BYTES)


PAIR_W = 2 * LANES
N_PAIRS = N_HEADS // 2
QA_W = N_PAIRS * PAIR_W
C3_ONE = 3 * N_HEADS
KT_ROWS = HEAD_DIM + BF16_ROWS
N_PIECES = 3


def _split3(c):
    hi = c.astype(BF16).astype(F32)
    r = c - hi
    mid = r.astype(BF16).astype(F32)
    lo = (r - mid).astype(BF16).astype(F32)
    return hi, mid, lo


def _aug_placement():
    e = [[0.0] * D_MODEL for _ in range(LANES)]
    for h in range(N_HEADS):
        base = LANES * (h // 2) + HEAD_DIM * (h % 2)
        for piece in range(N_PIECES):
            e[C3_ONE][base + piece] = 1.0
            e[piece * N_HEADS + h][base + N_PIECES + piece] = 1.0
    return jnp.array(e, BF16)


def _query_proj(x, g, wq_ref):
    xn = _rmsnorm(x, g).astype(BF16)
    return _dot(xn, wq_ref[...]) * (HEAD_DIM ** -0.5)


def _store_query(q, c3, e_ref, qa_ref):
    aug = _dot(c3, e_ref[...])
    for p in range(N_PAIRS):
        qa_ref[:, PAIR_W * p:PAIR_W * p + LANES] = q[:, LANES * p:LANES * (p + 1)].astype(BF16)
        qa_ref[:, PAIR_W * p + LANES:PAIR_W * (p + 1)] = aug[:, LANES * p:LANES * (p + 1)].astype(BF16)


def _ffn_kernel(*refs, has_attn, has_final, has_query, n_casts):
    it = iter(refs)
    h_ref = next(it)
    if has_attn:
        o_ref, wo_ref = next(it), next(it)
    g_ref, win_ref, wout_ref = next(it), next(it), next(it)
    if has_final:
        gf_ref = next(it)
    if has_query:
        gq_ref, wq_ref, c3_ref, e_ref = next(it), next(it), next(it), next(it)
    cast_srcs = [next(it) for _ in range(n_casts)]
    out_ref = next(it)
    qa_ref = next(it) if has_query else None
    _run_casts(cast_srcs, [next(it) for _ in range(n_casts)])

    x = h_ref[...]
    if has_attn:
        x = x + _dot(o_ref[...], wo_ref[...])
    xn = _rmsnorm(x, g_ref[...]).astype(BF16)
    acc = x
    for c0, cw in FFN_CHUNKS:
        gate = _dot(xn, win_ref[:, c0:c0 + cw])
        up = _dot(xn, win_ref[:, D_FF + c0:D_FF + c0 + cw])
        act = (gate * _sigmoid(gate)) * up
        acc = acc + _dot(act.astype(BF16), wout_ref[c0:c0 + cw, :])
    if has_final:
        acc = _rmsnorm(acc, gf_ref[...])
    out_ref[...] = acc
    if has_query:
        _store_query(_query_proj(acc, gq_ref[...], wq_ref), c3_ref[...], e_ref, qa_ref)


def _ffn(h2d, layer, norm_ffn, w_in, w_out, attn=None, final_g=None, query=None, casts=()):
    m = h2d.shape[0]
    tile = ROW_TILE
    row = pl.BlockSpec((tile, D_MODEL), lambda i: (i, 0))
    args = [h2d]
    specs = [row]
    if attn is not None:
        o2d, w_o, j = attn
        args += [o2d, w_o]
        specs += [row, _layer_spec((D_MODEL, D_MODEL), j)]
    args += [norm_ffn, w_in[0], w_out[0]]
    specs += [_layer_spec((1, D_MODEL), layer), _weight_spec((D_MODEL, 2 * D_FF), w_in[1]),
              _weight_spec((D_FF, D_MODEL), w_out[1])]
    if final_g is not None:
        args.append(final_g)
        specs.append(_const_spec((1, D_MODEL)))
    out_specs = row
    out_shape = jax.ShapeDtypeStruct((m, D_MODEL), F32)
    if query is not None:
        norm_mix, w_q, j, c3, e = query
        args += [norm_mix, w_q, c3, e]
        specs += [_layer_spec((1, D_MODEL), layer + 1), _layer_spec((D_MODEL, D_MODEL), j),
                  pl.BlockSpec((tile, LANES), lambda i: (i, 0)), _const_spec((LANES, D_MODEL))]
        out_specs = [row, pl.BlockSpec((tile, QA_W), lambda i: (i, 0))]
        out_shape = [out_shape, jax.ShapeDtypeStruct((m, QA_W), BF16)]
    if casts:
        jobs = [_cast_job(w, m // tile, lyr) for w, lyr in casts]
        args += [w for w, _ in casts]
        specs += [j[0] for j in jobs]
        out_specs = ([out_specs] if query is None else out_specs) + [j[1] for j in jobs]
        out_shape = ([out_shape] if query is None else out_shape) + [j[2] for j in jobs]
    return pl.pallas_call(
        functools.partial(_ffn_kernel, has_attn=attn is not None, has_final=final_g is not None,
                          has_query=query is not None, n_casts=len(casts)),
        grid=(m // tile,),
        in_specs=specs,
        out_specs=out_specs,
        out_shape=out_shape,
        compiler_params=_params("parallel"),
        name="ffn",
    )(*args)


REC_STEPS = 64
REC_ROWS = REC_STEPS * SUBLANES
HALO_ROWS = (CONV_W - 1) * SUBLANES


def _rec_kernel(*refs, n_casts, batch_major_in, batch_major_out):
    (h_ref, g_ref, win_ref, cw_ref, cb_ref, wg_ref, bg_ref, lam_ref, wout_ref), refs = refs[:9], refs[9:]
    cast_srcs, out_ref, cast_dsts = refs[:n_casts], refs[n_casts], refs[n_casts + 1:2 * n_casts + 1]
    x_scr, rec_scr, a_scr, u_scr, carry_scr = refs[2 * n_casts + 1:]
    rows = REC_ROWS
    _run_casts(cast_srcs, cast_dsts)

    @pl.when(pl.program_id(0) == 0)
    def _():
        rec_scr[0:HALO_ROWS, :] = jnp.zeros((HALO_ROWS, D_RNN), F32)
        carry_scr[...] = jnp.zeros((SUBLANES, D_RNN), F32)

    if batch_major_in:
        for t in range(REC_STEPS):
            x_scr[t * SUBLANES:(t + 1) * SUBLANES, :] = h_ref[:, t, :]
        x = x_scr[...]
    else:
        x = h_ref[...]
    xn = _rmsnorm(x, g_ref[...]).astype(BF16)
    proj = _dot(xn, win_ref[...])
    rec_scr[HALO_ROWS:HALO_ROWS + rows, :] = proj[:, D_RNN:]

    conv = cb_ref[...]
    for tap in range(CONV_W):
        conv = conv + rec_scr[tap * SUBLANES:tap * SUBLANES + rows, :] * cw_ref[tap:tap + 1, :]
    rec_scr[0:HALO_ROWS, :] = rec_scr[rows:rows + HALO_ROWS, :]

    for n in range(LRU_BLOCKS):
        lo = n * LRU_BLOCK_W
        xb = conv[:, lo:lo + LRU_BLOCK_W]
        half_gates = _dot(xb.astype(BF16), wg_ref[n]) + bg_ref[n]
        gate_i = _sigmoid_of_twice(half_gates[:, :LRU_BLOCK_W])
        gate_r = _sigmoid_of_twice(half_gates[:, LRU_BLOCK_W:])
        log_a = gate_r * (-LRU_C * _softplus(-lam_ref[:, lo:lo + LRU_BLOCK_W]))
        a = jnp.exp(log_a)
        a_scr[:, lo:lo + LRU_BLOCK_W] = a
        one_m_a2 = 1.0 - a * a
        mult = one_m_a2 * lax.rsqrt(jnp.maximum(one_m_a2, TINY))
        u_scr[:, lo:lo + LRU_BLOCK_W] = xb * gate_i * mult

    hs = carry_scr[...]
    for t in range(REC_STEPS):
        r0 = t * SUBLANES
        hs = a_scr[r0:r0 + SUBLANES, :] * hs + u_scr[r0:r0 + SUBLANES, :]
        u_scr[r0:r0 + SUBLANES, :] = hs
    carry_scr[...] = hs
    y = _gelu_tanh(proj[:, :D_RNN]) * u_scr[...]
    res = x + _dot(y.astype(BF16), wout_ref[...])
    if batch_major_out:
        x_scr[...] = res
        for t in range(REC_STEPS):
            out_ref[:, t, :] = x_scr[t * SUBLANES:(t + 1) * SUBLANES, :]
    else:
        out_ref[...] = res


def _rec_layer(h, i, norm_mix, w_in, conv_w, conv_b, w_gates, b_gates, lam, w_out, *, seq_len,
               batch_major_out, casts=()):
    batch_major_in = h.ndim == 3
    b, s = SUBLANES, seq_len
    assert h.size == b * s * D_MODEL, "the (time, batch) row layout needs one batch row per sublane"
    rows = REC_ROWS
    batch_major = pl.BlockSpec((b, REC_STEPS, D_MODEL), lambda si: (0, si, 0))
    time_major = pl.BlockSpec((rows, D_MODEL), lambda si: (si, 0))
    out_shape = (b, s, D_MODEL) if batch_major_out else (s * b, D_MODEL)
    jobs = [_cast_job(w, s // REC_STEPS, lyr) for w, lyr in casts]
    return pl.pallas_call(
        functools.partial(_rec_kernel, n_casts=len(casts), batch_major_in=batch_major_in,
                          batch_major_out=batch_major_out),
        grid=(s // REC_STEPS,),
        in_specs=[
            batch_major if batch_major_in else time_major,
            _layer_spec((1, D_MODEL), i),
            _layer_spec((D_MODEL, 2 * D_RNN), i),
            _layer_spec((CONV_W, D_RNN), i),
            _layer_spec((1, D_RNN), i),
            _layer_spec((LRU_BLOCKS, LRU_BLOCK_W, 2 * LRU_BLOCK_W), i),
            _layer_spec((LRU_BLOCKS, 1, 2 * LRU_BLOCK_W), i),
            _layer_spec((1, D_RNN), i),
            _layer_spec((D_RNN, D_MODEL), i),
        ] + [j[0] for j in jobs],
        out_specs=[batch_major if batch_major_out else time_major] + [j[1] for j in jobs],
        out_shape=[jax.ShapeDtypeStruct(out_shape, F32)] + [j[2] for j in jobs],
        scratch_shapes=[
            pltpu.VMEM((rows, D_MODEL), F32),
            pltpu.VMEM((rows + HALO_ROWS, D_RNN), F32),
            pltpu.VMEM((rows, D_RNN), F32),
            pltpu.VMEM((rows, D_RNN), F32),
            pltpu.VMEM((SUBLANES, D_RNN), F32),
        ],
        compiler_params=_params("arbitrary"),
        name="rec_layer",
    )(h, norm_mix, w_in, conv_w, conv_b, w_gates, b_gates, lam, w_out, *[w for w, _ in casts])


def _kvf_kernel(h_ref, g_ref, wkv_ref, wf_ref, bf_ref, gq_ref, wq_ref, e_ref,
                kt_ref, v_ref, c3_ref, qa_ref, carry_scr):
    ts = SEQ_TILE

    @pl.when(pl.program_id(1) == 0)
    def _():
        carry_scr[...] = jnp.zeros((1, LANES), F32)

    x = h_ref[0]
    xn = _rmsnorm(x, g_ref[...]).astype(BF16)
    kv = _dot(xn, wkv_ref[...])

    f_logit = _dot(xn, wf_ref[...]) + bf_ref[...]
    c = -_softplus(-f_logit)
    row = lax.broadcasted_iota(jnp.int32, (ts, LANES), 0)
    k = 1
    while k < ts:
        c = c + jnp.where(row >= k, pltpu.roll(c, k, 0), 0.0)
        k *= 2
    c = c + carry_scr[...]
    carry_scr[...] = c[ts - 1:ts, :]

    lane = lax.broadcasted_iota(jnp.int32, (ts, LANES), 1)
    hi, mid, lo = _split3(c)
    c3 = jnp.where(lane < N_HEADS, hi,
                   jnp.where(lane < 2 * N_HEADS, pltpu.roll(mid, N_HEADS, 1),
                             jnp.where(lane < C3_ONE, pltpu.roll(lo, 2 * N_HEADS, 1),
                                       jnp.where(lane == C3_ONE, 1.0, 0.0)))).astype(BF16)
    c3_ref[0] = c3
    _store_query(_query_proj(x, gq_ref[...], wq_ref), c3, e_ref, qa_ref.at[0])

    v_ref[0] = kv[:, D_MODEL:].astype(BF16)
    c_t = c.T
    arow = lax.broadcasted_iota(jnp.int32, (BF16_ROWS, ts), 0)
    for p in range(N_PAIRS):
        k_t = kv[:, LANES * p:LANES * (p + 1)].T
        for j in range(2):
            h = 2 * p + j
            kt_ref[0, h, 0:HEAD_DIM, :] = k_t[HEAD_DIM * j:HEAD_DIM * (j + 1), :].astype(BF16)
            chi, cmid, clo = _split3(c_t[h:h + 1, :])
            aug = jnp.where(arow == 0, -chi, jnp.where(arow == 1, -cmid, jnp.where(arow == 2, -clo,
                            jnp.where(arow < 2 * N_PIECES, 1.0, 0.0))))
            kt_ref[0, h, HEAD_DIM:KT_ROWS, :] = aug.astype(BF16)


def _kvf(h, norm_kv, w_kv, w_f, b_f, norm_mix, w_q, e):
    b, s, _ = h.shape
    ts = SEQ_TILE
    return pl.pallas_call(
        _kvf_kernel,
        grid=(b, s // ts),
        in_specs=[
            pl.BlockSpec((1, ts, D_MODEL), lambda bi, si: (bi, si, 0)),
            _const_spec((1, D_MODEL)),
            _const_spec((D_MODEL, 2 * D_MODEL)),
            _const_spec((D_MODEL, LANES)),
            _const_spec((1, LANES)),
            _layer_spec((1, D_MODEL), N_A),
            _layer_spec((D_MODEL, D_MODEL), 0),
            _const_spec((LANES, D_MODEL)),
        ],
        out_specs=[
            pl.BlockSpec((1, N_HEADS, KT_ROWS, ts), lambda bi, si: (bi, 0, 0, si)),
            pl.BlockSpec((1, ts, D_MODEL), lambda bi, si: (bi, si, 0)),
            pl.BlockSpec((1, ts, LANES), lambda bi, si: (bi, si, 0)),
            pl.BlockSpec((1, ts, QA_W), lambda bi, si: (bi, si, 0)),
        ],
        out_shape=[
            jax.ShapeDtypeStruct((b, N_HEADS, KT_ROWS, s), BF16),
            jax.ShapeDtypeStruct((b, s, D_MODEL), BF16),
            jax.ShapeDtypeStruct((b, s, LANES), BF16),
            jax.ShapeDtypeStruct((b, s, QA_W), BF16),
        ],
        scratch_shapes=[pltpu.VMEM((1, LANES), F32)],
        compiler_params=_params("parallel", "arbitrary"),
        name="kvf",
    )(h, norm_kv, w_kv, w_f, b_f, norm_mix, w_q, e)


ATT_PAIRS = 2


def _attn_kernel(qa_ref, ktc_ref, v_ref, o_ref, kt_scr, va_scr):
    t = ATT_TILE
    s_len = qa_ref.shape[1]
    tri = (lax.broadcasted_iota(jnp.int32, (t, t), 0) >= lax.broadcasted_iota(jnp.int32, (t, t), 1))
    lane = lax.broadcasted_iota(jnp.int32, (t, LANES), 1)

    @pl.when((pl.program_id(0) == 0) & (pl.program_id(1) == 0))
    def _():
        kt_scr[...] = jnp.zeros(kt_scr.shape, BF16)
        ones_lane = lax.broadcasted_iota(jnp.int32, (s_len, LANES), 1) == 0
        for p in range(ATT_PAIRS):
            va_scr[p, :, LANES:PAIR_W] = jnp.where(ones_lane, 1.0, 0.0).astype(BF16)

    for j in range(2 * ATT_PAIRS):
        own = HEAD_DIM * (j % 2)
        kt_scr[j, own:own + HEAD_DIM, :] = ktc_ref[0, j, 0:HEAD_DIM, :]
        kt_scr[j, LANES + own:LANES + own + BF16_ROWS, :] = ktc_ref[0, j, HEAD_DIM:KT_ROWS, :]
    for p in range(ATT_PAIRS):
        va_scr[p, :, 0:LANES] = v_ref[0, :, LANES * p:LANES * (p + 1)]

    def logits(i, j):
        lo = i * t
        p = j // 2
        s = _dot(qa_ref[0, lo:lo + t, PAIR_W * p:PAIR_W * (p + 1)], kt_scr[j, :, 0:lo + t])
        s_diag = jnp.where(tri, s[:, lo:], -jnp.inf)
        s = jnp.concatenate([s[:, :lo], s_diag], axis=1) if i > 0 else s_diag
        return s, jnp.max(s, axis=-1, keepdims=True)

    def weighted_values(i, j, s, m):
        acc = _dot(jnp.exp(s - m).astype(BF16), va_scr[j // 2, 0:(i + 1) * t, :])
        return acc[:, :LANES] * (1.0 / acc[:, LANES:LANES + 1])

    units = [(i, j) for i in range(s_len // t) for j in range(2 * ATT_PAIRS)]
    pending = logits(*units[0])
    even = None
    for n, (i, j) in enumerate(units):
        ahead = logits(*units[n + 1]) if n + 1 < len(units) else None
        out = weighted_values(i, j, *pending)
        pending = ahead
        if j % 2 == 0:
            even = out
        else:
            p = j // 2
            o_ref[0, i * t:(i + 1) * t, LANES * p:LANES * (p + 1)] = (
                jnp.where(lane < HEAD_DIM, even, out).astype(BF16))


def _attention(qa, ktc, v):
    b, s, _ = qa.shape
    return pl.pallas_call(
        _attn_kernel,
        grid=(b, N_PAIRS // ATT_PAIRS),
        in_specs=[
            pl.BlockSpec((1, s, PAIR_W * ATT_PAIRS), lambda bi, p: (bi, 0, p)),
            pl.BlockSpec((1, 2 * ATT_PAIRS, KT_ROWS, s), lambda bi, p: (bi, p, 0, 0)),
            pl.BlockSpec((1, s, LANES * ATT_PAIRS), lambda bi, p: (bi, 0, p)),
        ],
        out_specs=pl.BlockSpec((1, s, LANES * ATT_PAIRS), lambda bi, p: (bi, 0, p)),
        out_shape=jax.ShapeDtypeStruct((b, s, D_MODEL), BF16),
        scratch_shapes=[
            pltpu.VMEM((2 * ATT_PAIRS, PAIR_W, s), BF16),
            pltpu.VMEM((ATT_PAIRS, s, PAIR_W), BF16),
        ],
        compiler_params=_params("arbitrary", "arbitrary"),
        name="fox_attention",
    )(qa, ktc, v)


def kernel(x, norm_mix, norm_ffn, w_ffn_in, w_ffn_out, w_rec_in, conv_w, conv_b, w_lru_gates,
           b_lru_gates, lru_param, w_rec_out, norm_kv, w_kvf, b_forget, w_q, w_o, norm_final):
    b, s, d = x.shape
    m = b * s
    row = lambda a: a.reshape(1, -1)
    rows = lambda a: a.reshape(a.shape[0], 1, a.shape[1])

    norm_mix, norm_ffn = rows(norm_mix), rows(norm_ffn)
    w_rec_in, w_rec_out = w_rec_in.astype(BF16), w_rec_out.astype(BF16)
    w_lru_gates = (0.5 * w_lru_gates).astype(BF16)
    b_lru_gates = (0.5 * b_lru_gates).reshape(N_A, LRU_BLOCKS, 1, 2 * LRU_BLOCK_W)
    conv_b, lru_param = rows(conv_b), rows(lru_param)

    h = x
    for i in range(N_A):
        last = i == N_A - 1
        rec = _rec_layer(h, i, norm_mix, w_rec_in, conv_w, conv_b, w_lru_gates, b_lru_gates, lru_param, w_rec_out,
                         seq_len=s, batch_major_out=last, casts=((w_ffn_in, 0), (w_ffn_out, 0)) if i == 0 else ())
        if i == 0:
            h, fin0, fout0 = rec
            h, w_ffn_in, w_ffn_out = _ffn(h.reshape(m, d), i, norm_ffn, (fin0, None), (fout0, None),
                                          casts=((w_ffn_in, None), (w_ffn_out, None)))
        elif last:
            h, w_kvf, w_q, w_o = _ffn(rec[0].reshape(m, d), i, norm_ffn, (w_ffn_in, i), (w_ffn_out, i),
                                      casts=((w_kvf, None), (w_q, None), (w_o, None)))
        else:
            h = _ffn(rec[0].reshape(m, d), i, norm_ffn, (w_ffn_in, i), (w_ffn_out, i))
    h = h.reshape(b, s, d)

    w_f = jnp.pad(w_kvf[:, 2 * D_MODEL:], ((0, 0), (0, LANES - N_HEADS)))
    b_f = jnp.pad(b_forget, (0, LANES - N_HEADS)).reshape(1, LANES)
    e = _aug_placement()
    kt, va, c3, qa = _kvf(h, row(norm_kv), w_kvf, w_f, b_f, norm_mix, w_q, e)
    c3 = c3.reshape(m, LANES)

    h2d = h.reshape(m, d)
    for j in range(DEPTH - N_A):
        layer = N_A + j
        o = _attention(qa, kt, va)
        attn = (o.reshape(m, d), w_o, j)
        if layer == DEPTH - 1:
            h2d = _ffn(h2d, layer, norm_ffn, (w_ffn_in, layer), (w_ffn_out, layer), attn=attn,
                       final_g=row(norm_final))
        else:
            h2d, qa = _ffn(h2d, layer, norm_ffn, (w_ffn_in, layer), (w_ffn_out, layer), attn=attn,
                           query=(norm_mix, w_q, j + 1, c3, e))
            qa = qa.reshape(b, s, QA_W)
    return h2d.reshape(b, s, d)
```

```python
import functools
import math

import jax
import jax.numpy as jnp
from jax import lax
from jax.experimental import pallas as pl
from jax.experimental.pallas import tpu as pltpu

D_MODEL = 1024
DEPTH = 4
N_A = DEPTH // 2
D_RNN = D_MODEL
LRU_BLOCK_W = 256
LRU_BLOCKS = D_RNN // LRU_BLOCK_W
CONV_W = 4
LRU_C = 8.0
N_HEADS = 16
HEAD_DIM = D_MODEL // N_HEADS
D_FF = 2816
EPS = 1e-6

SUBLANES = 8
LANES = 128
BF16_ROWS = 2 * SUBLANES
VMEM_LIMIT_BYTES = 56 * 1024 * 1024
TINY = 1e-30

F32 = jnp.float32
BF16 = jnp.bfloat16

ROW_TILE = 512
SEQ_TILE = 512
ATT_TILE = 256
FFN_CHUNKS = ((0, 512), (512, 512), (1024, 512), (1536, 512), (2048, 512), (2560, 256))


def _rmsnorm(x, g):
    y = x * lax.rsqrt(jnp.mean(x * x, axis=-1, keepdims=True) + EPS)
    return y * g


def _sigmoid(x):
    return 1.0 / (1.0 + jnp.exp(-x))


def _softplus(x):
    return jnp.maximum(x, 0.0) + jnp.log1p(jnp.exp(-jnp.abs(x)))


def _sigmoid_of_twice(half_x):
    return 0.5 * jnp.tanh(half_x) + 0.5


def _gelu_tanh(x):
    c = math.sqrt(2.0 / math.pi)
    half = 0.5 * x
    return half + half * jnp.tanh(x * (c + (c * 0.044715) * (x * x)))


def _dot(a, b):
    return jnp.dot(a, b, preferred_element_type=F32)


def _const_spec(shape):
    n = len(shape)
    return pl.BlockSpec(shape, lambda *_: (0,) * n, pipeline_mode=pl.Buffered(1))


def _layer_spec(shape, layer):
    n = len(shape)
    return pl.BlockSpec((None,) + tuple(shape), lambda *_: (layer,) + (0,) * n, pipeline_mode=pl.Buffered(1))


def _weight_spec(shape, layer):
    return _const_spec(shape) if layer is None else _layer_spec(shape, layer)


def _cast_job(w, steps, layer):
    r, c = w.shape[-2:]
    hold = next(k for k in (1, 2, 4, 8) if r % (steps // k) == 0 and (r // (steps // k)) % BF16_ROWS == 0)
    rt = r // (steps // hold)
    if w.ndim == 2:
        spec = pl.BlockSpec((rt, c), lambda i: (i // hold, 0))
        return spec, spec, jax.ShapeDtypeStruct(w.shape, BF16)
    n_layers = w.shape[0]
    if layer is None:
        return (pl.BlockSpec((n_layers, rt, c), lambda i: (0, i // hold, 0)),
                pl.BlockSpec((n_layers, rt, c), lambda i: (0, i // hold, 0)),
                jax.ShapeDtypeStruct(w.shape, BF16))
    return (pl.BlockSpec((None, rt, c), lambda i: (layer, i // hold, 0)),
            pl.BlockSpec((rt, c), lambda i: (i // hold, 0)),
            jax.ShapeDtypeStruct((r, c), BF16))


def _run_casts(srcs, dsts):
    for src, dst in zip(srcs, dsts):
        dst[...] = src[...].astype(BF16)


def _params(*sem):
    return pltpu.CompilerParams(dimension_semantics=sem, vmem_limit_bytes=VMEM_LIMIT_BYTES)


PAIR_W = 2 * LANES
N_PAIRS = N_HEADS // 2
QA_W = N_PAIRS * PAIR_W
C3_ONE = 3 * N_HEADS
KT_ROWS = HEAD_DIM + BF16_ROWS
N_PIECES = 3


def _split3(c):
    hi = c.astype(BF16).astype(F32)
    r = c - hi
    mid = r.astype(BF16).astype(F32)
    lo = (r - mid).astype(BF16).astype(F32)
    return hi, mid, lo


def _aug_placement():
    e = [[0.0] * D_MODEL for _ in range(LANES)]
    for h in range(N_HEADS):
        base = LANES * (h // 2) + HEAD_DIM * (h % 2)
        for piece in range(N_PIECES):
            e[C3_ONE][base + piece] = 1.0
            e[piece * N_HEADS + h][base + N_PIECES + piece] = 1.0
    return jnp.array(e, BF16)


def _query_proj(x, g, wq_ref):
    xn = _rmsnorm(x, g).astype(BF16)
    return _dot(xn, wq_ref[...]) * (HEAD_DIM ** -0.5)


def _store_query(q, c3, e_ref, qa_ref):
    aug = _dot(c3, e_ref[...])
    for p in range(N_PAIRS):
        qa_ref[:, PAIR_W * p:PAIR_W * p + LANES] = q[:, LANES * p:LANES * (p + 1)].astype(BF16)
        qa_ref[:, PAIR_W * p + LANES:PAIR_W * (p + 1)] = aug[:, LANES * p:LANES * (p + 1)].astype(BF16)


def _ffn_kernel(*refs, has_attn, has_final, has_query, n_casts):
    it = iter(refs)
    h_ref = next(it)
    if has_attn:
        o_ref, wo_ref = next(it), next(it)
    g_ref, win_ref, wout_ref = next(it), next(it), next(it)
    if has_final:
        gf_ref = next(it)
    if has_query:
        gq_ref, wq_ref, c3_ref, e_ref = next(it), next(it), next(it), next(it)
    cast_srcs = [next(it) for _ in range(n_casts)]
    out_ref = next(it)
    qa_ref = next(it) if has_query else None
    _run_casts(cast_srcs, [next(it) for _ in range(n_casts)])

    x = h_ref[...]
    if has_attn:
        x = x + _dot(o_ref[...], wo_ref[...])
    xn = _rmsnorm(x, g_ref[...]).astype(BF16)
    acc = x
    for c0, cw in FFN_CHUNKS:
        gate = _dot(xn, win_ref[:, c0:c0 + cw])
        up = _dot(xn, win_ref[:, D_FF + c0:D_FF + c0 + cw])
        act = (gate * _sigmoid(gate)) * up
        acc = acc + _dot(act.astype(BF16), wout_ref[c0:c0 + cw, :])
    if has_final:
        acc = _rmsnorm(acc, gf_ref[...])
    out_ref[...] = acc
    if has_query:
        _store_query(_query_proj(acc, gq_ref[...], wq_ref), c3_ref[...], e_ref, qa_ref)


def _ffn(h2d, layer, norm_ffn, w_in, w_out, attn=None, final_g=None, query=None, casts=()):
    m = h2d.shape[0]
    tile = ROW_TILE
    row = pl.BlockSpec((tile, D_MODEL), lambda i: (i, 0))
    args = [h2d]
    specs = [row]
    if attn is not None:
        o2d, w_o, j = attn
        args += [o2d, w_o]
        specs += [row, _layer_spec((D_MODEL, D_MODEL), j)]
    args += [norm_ffn, w_in[0], w_out[0]]
    specs += [_layer_spec((1, D_MODEL), layer), _weight_spec((D_MODEL, 2 * D_FF), w_in[1]),
              _weight_spec((D_FF, D_MODEL), w_out[1])]
    if final_g is not None:
        args.append(final_g)
        specs.append(_const_spec((1, D_MODEL)))
    out_specs = row
    out_shape = jax.ShapeDtypeStruct((m, D_MODEL), F32)
    if query is not None:
        norm_mix, w_q, j, c3, e = query
        args += [norm_mix, w_q, c3, e]
        specs += [_layer_spec((1, D_MODEL), layer + 1), _layer_spec((D_MODEL, D_MODEL), j),
                  pl.BlockSpec((tile, LANES), lambda i: (i, 0)), _const_spec((LANES, D_MODEL))]
        out_specs = [row, pl.BlockSpec((tile, QA_W), lambda i: (i, 0))]
        out_shape = [out_shape, jax.ShapeDtypeStruct((m, QA_W), BF16)]
    if casts:
        jobs = [_cast_job(w, m // tile, lyr) for w, lyr in casts]
        args += [w for w, _ in casts]
        specs += [j[0] for j in jobs]
        out_specs = ([out_specs] if query is None else out_specs) + [j[1] for j in jobs]
        out_shape = ([out_shape] if query is None else out_shape) + [j[2] for j in jobs]
    return pl.pallas_call(
        functools.partial(_ffn_kernel, has_attn=attn is not None, has_final=final_g is not None,
                          has_query=query is not None, n_casts=len(casts)),
        grid=(m // tile,),
        in_specs=specs,
        out_specs=out_specs,
        out_shape=out_shape,
        compiler_params=_params("parallel"),
        name="ffn",
    )(*args)


REC_STEPS = 64
REC_ROWS = REC_STEPS * SUBLANES
HALO_ROWS = (CONV_W - 1) * SUBLANES


def _rec_kernel(*refs, n_casts, batch_major_in, batch_major_out):
    (h_ref, g_ref, win_ref, cw_ref, cb_ref, wg_ref, bg_ref, lam_ref, wout_ref), refs = refs[:9], refs[9:]
    cast_srcs, out_ref, cast_dsts = refs[:n_casts], refs[n_casts], refs[n_casts + 1:2 * n_casts + 1]
    x_scr, rec_scr, a_scr, u_scr, carry_scr = refs[2 * n_casts + 1:]
    rows = REC_ROWS
    _run_casts(cast_srcs, cast_dsts)

    @pl.when(pl.program_id(0) == 0)
    def _():
        rec_scr[0:HALO_ROWS, :] = jnp.zeros((HALO_ROWS, D_RNN), F32)
        carry_scr[...] = jnp.zeros((SUBLANES, D_RNN), F32)

    if batch_major_in:
        for t in range(REC_STEPS):
            x_scr[t * SUBLANES:(t + 1) * SUBLANES, :] = h_ref[:, t, :]
        x = x_scr[...]
    else:
        x = h_ref[...]
    xn = _rmsnorm(x, g_ref[...]).astype(BF16)
    proj = _dot(xn, win_ref[...])
    rec_scr[HALO_ROWS:HALO_ROWS + rows, :] = proj[:, D_RNN:]

    conv = cb_ref[...]
    for tap in range(CONV_W):
        conv = conv + rec_scr[tap * SUBLANES:tap * SUBLANES + rows, :] * cw_ref[tap:tap + 1, :]
    rec_scr[0:HALO_ROWS, :] = rec_scr[rows:rows + HALO_ROWS, :]

    for n in range(LRU_BLOCKS):
        lo = n * LRU_BLOCK_W
        xb = conv[:, lo:lo + LRU_BLOCK_W]
        half_gates = _dot(xb.astype(BF16), wg_ref[n]) + bg_ref[n]
        gate_i = _sigmoid_of_twice(half_gates[:, :LRU_BLOCK_W])
        gate_r = _sigmoid_of_twice(half_gates[:, LRU_BLOCK_W:])
        log_a = gate_r * (-LRU_C * _softplus(-lam_ref[:, lo:lo + LRU_BLOCK_W]))
        a = jnp.exp(log_a)
        a_scr[:, lo:lo + LRU_BLOCK_W] = a
        one_m_a2 = 1.0 - a * a
        mult = one_m_a2 * lax.rsqrt(jnp.maximum(one_m_a2, TINY))
        u_scr[:, lo:lo + LRU_BLOCK_W] = xb * gate_i * mult

    hs = carry_scr[...]
    for t in range(REC_STEPS):
        r0 = t * SUBLANES
        hs = a_scr[r0:r0 + SUBLANES, :] * hs + u_scr[r0:r0 + SUBLANES, :]
        u_scr[r0:r0 + SUBLANES, :] = hs
    carry_scr[...] = hs
    y = _gelu_tanh(proj[:, :D_RNN]) * u_scr[...]
    res = x + _dot(y.astype(BF16), wout_ref[...])
    if batch_major_out:
        x_scr[...] = res
        for t in range(REC_STEPS):
            out_ref[:, t, :] = x_scr[t * SUBLANES:(t + 1) * SUBLANES, :]
    else:
        out_ref[...] = res


def _rec_layer(h, i, norm_mix, w_in, conv_w, conv_b, w_gates, b_gates, lam, w_out, *, seq_len,
               batch_major_out, casts=()):
    batch_major_in = h.ndim == 3
    b, s = SUBLANES, seq_len
    assert h.size == b * s * D_MODEL, "the (time, batch) row layout needs one batch row per sublane"
    rows = REC_ROWS
    batch_major = pl.BlockSpec((b, REC_STEPS, D_MODEL), lambda si: (0, si, 0))
    time_major = pl.BlockSpec((rows, D_MODEL), lambda si: (si, 0))
    out_shape = (b, s, D_MODEL) if batch_major_out else (s * b, D_MODEL)
    jobs = [_cast_job(w, s // REC_STEPS, lyr) for w, lyr in casts]
    return pl.pallas_call(
        functools.partial(_rec_kernel, n_casts=len(casts), batch_major_in=batch_major_in,
                          batch_major_out=batch_major_out),
        grid=(s // REC_STEPS,),
        in_specs=[
            batch_major if batch_major_in else time_major,
            _layer_spec((1, D_MODEL), i),
            _layer_spec((D_MODEL, 2 * D_RNN), i),
            _layer_spec((CONV_W, D_RNN), i),
            _layer_spec((1, D_RNN), i),
            _layer_spec((LRU_BLOCKS, LRU_BLOCK_W, 2 * LRU_BLOCK_W), i),
            _layer_spec((LRU_BLOCKS, 1, 2 * LRU_BLOCK_W), i),
            _layer_spec((1, D_RNN), i),
            _layer_spec((D_RNN, D_MODEL), i),
        ] + [j[0] for j in jobs],
        out_specs=[batch_major if batch_major_out else time_major] + [j[1] for j in jobs],
        out_shape=[jax.ShapeDtypeStruct(out_shape, F32)] + [j[2] for j in jobs],
        scratch_shapes=[
            pltpu.VMEM((rows, D_MODEL), F32),
            pltpu.VMEM((rows + HALO_ROWS, D_RNN), F32),
            pltpu.VMEM((rows, D_RNN), F32),
            pltpu.VMEM((rows, D_RNN), F32),
            pltpu.VMEM((SUBLANES, D_RNN), F32),
        ],
        compiler_params=_params("arbitrary"),
        name="rec_layer",
    )(h, norm_mix, w_in, conv_w, conv_b, w_gates, b_gates, lam, w_out, *[w for w, _ in casts])


def _kvf_kernel(h_ref, g_ref, wkv_ref, wf_ref, bf_ref, gq_ref, wq_ref, e_ref,
                kt_ref, v_ref, c3_ref, qa_ref, carry_scr):
    ts = SEQ_TILE

    @pl.when(pl.program_id(1) == 0)
    def _():
        carry_scr[...] = jnp.zeros((1, LANES), F32)

    x = h_ref[0]
    xn = _rmsnorm(x, g_ref[...]).astype(BF16)
    kv = _dot(xn, wkv_ref[...])

    f_logit = _dot(xn, wf_ref[...]) + bf_ref[...]
    c = -_softplus(-f_logit)
    row = lax.broadcasted_iota(jnp.int32, (ts, LANES), 0)
    k = 1
    while k < ts:
        c = c + jnp.where(row >= k, pltpu.roll(c, k, 0), 0.0)
        k *= 2
    c = c + carry_scr[...]
    carry_scr[...] = c[ts - 1:ts, :]

    lane = lax.broadcasted_iota(jnp.int32, (ts, LANES), 1)
    hi, mid, lo = _split3(c)
    c3 = jnp.where(lane < N_HEADS, hi,
                   jnp.where(lane < 2 * N_HEADS, pltpu.roll(mid, N_HEADS, 1),
                             jnp.where(lane < C3_ONE, pltpu.roll(lo, 2 * N_HEADS, 1),
                                       jnp.where(lane == C3_ONE, 1.0, 0.0)))).astype(BF16)
    c3_ref[0] = c3
    _store_query(_query_proj(x, gq_ref[...], wq_ref), c3, e_ref, qa_ref.at[0])

    v_ref[0] = kv[:, D_MODEL:].astype(BF16)
    c_t = c.T
    arow = lax.broadcasted_iota(jnp.int32, (BF16_ROWS, ts), 0)
    for p in range(N_PAIRS):
        k_t = kv[:, LANES * p:LANES * (p + 1)].T
        for j in range(2):
            h = 2 * p + j
            kt_ref[0, h, 0:HEAD_DIM, :] = k_t[HEAD_DIM * j:HEAD_DIM * (j + 1), :].astype(BF16)
            chi, cmid, clo = _split3(c_t[h:h + 1, :])
            aug = jnp.where(arow == 0, -chi, jnp.where(arow == 1, -cmid, jnp.where(arow == 2, -clo,
                            jnp.where(arow < 2 * N_PIECES, 1.0, 0.0))))
            kt_ref[0, h, HEAD_DIM:KT_ROWS, :] = aug.astype(BF16)


def _kvf(h, norm_kv, w_kv, w_f, b_f, norm_mix, w_q, e):
    b, s, _ = h.shape
    ts = SEQ_TILE
    return pl.pallas_call(
        _kvf_kernel,
        grid=(b, s // ts),
        in_specs=[
            pl.BlockSpec((1, ts, D_MODEL), lambda bi, si: (bi, si, 0)),
            _const_spec((1, D_MODEL)),
            _const_spec((D_MODEL, 2 * D_MODEL)),
            _const_spec((D_MODEL, LANES)),
            _const_spec((1, LANES)),
            _layer_spec((1, D_MODEL), N_A),
            _layer_spec((D_MODEL, D_MODEL), 0),
            _const_spec((LANES, D_MODEL)),
        ],
        out_specs=[
            pl.BlockSpec((1, N_HEADS, KT_ROWS, ts), lambda bi, si: (bi, 0, 0, si)),
            pl.BlockSpec((1, ts, D_MODEL), lambda bi, si: (bi, si, 0)),
            pl.BlockSpec((1, ts, LANES), lambda bi, si: (bi, si, 0)),
            pl.BlockSpec((1, ts, QA_W), lambda bi, si: (bi, si, 0)),
        ],
        out_shape=[
            jax.ShapeDtypeStruct((b, N_HEADS, KT_ROWS, s), BF16),
            jax.ShapeDtypeStruct((b, s, D_MODEL), BF16),
            jax.ShapeDtypeStruct((b, s, LANES), BF16),
            jax.ShapeDtypeStruct((b, s, QA_W), BF16),
        ],
        scratch_shapes=[pltpu.VMEM((1, LANES), F32)],
        compiler_params=_params("parallel", "arbitrary"),
        name="kvf",
    )(h, norm_kv, w_kv, w_f, b_f, norm_mix, w_q, e)


ATT_PAIRS = 4


def _attn_kernel(qa_ref, ktc_ref, v_ref, o_ref, kt_scr, va_scr):
    t = ATT_TILE
    s_len = qa_ref.shape[1]
    tri = (lax.broadcasted_iota(jnp.int32, (t, t), 0) >= lax.broadcasted_iota(jnp.int32, (t, t), 1))
    lane = lax.broadcasted_iota(jnp.int32, (t, LANES), 1)

    @pl.when((pl.program_id(0) == 0) & (pl.program_id(1) == 0))
    def _():
        kt_scr[...] = jnp.zeros(kt_scr.shape, BF16)
        ones_lane = lax.broadcasted_iota(jnp.int32, (s_len, LANES), 1) == 0
        for p in range(ATT_PAIRS):
            va_scr[p, :, LANES:PAIR_W] = jnp.where(ones_lane, 1.0, 0.0).astype(BF16)

    for j in range(2 * ATT_PAIRS):
        own = HEAD_DIM * (j % 2)
        kt_scr[j, own:own + HEAD_DIM, :] = ktc_ref[0, j, 0:HEAD_DIM, :]
        kt_scr[j, LANES + own:LANES + own + BF16_ROWS, :] = ktc_ref[0, j, HEAD_DIM:KT_ROWS, :]
    for p in range(ATT_PAIRS):
        va_scr[p, :, 0:LANES] = v_ref[0, :, LANES * p:LANES * (p + 1)]

    def logits(i, j):
        lo = i * t
        p = j // 2
        s = _dot(qa_ref[0, lo:lo + t, PAIR_W * p:PAIR_W * (p + 1)], kt_scr[j, :, 0:lo + t])
        s_diag = jnp.where(tri, s[:, lo:], -jnp.inf)
        s = jnp.concatenate([s[:, :lo], s_diag], axis=1) if i > 0 else s_diag
        return s, jnp.max(s, axis=-1, keepdims=True)

    def weighted_values(i, j, s, m):
        acc = _dot(jnp.exp(s - m).astype(BF16), va_scr[j // 2, 0:(i + 1) * t, :])
        return acc[:, :LANES] * (1.0 / acc[:, LANES:LANES + 1])

    units = [(i, j) for i in range(s_len // t) for j in range(2 * ATT_PAIRS)]
    pending = logits(*units[0])
    even = None
    for n, (i, j) in enumerate(units):
        ahead = logits(*units[n + 1]) if n + 1 < len(units) else None
        out = weighted_values(i, j, *pending)
        pending = ahead
        if j % 2 == 0:
            even = out
        else:
            p = j // 2
            o_ref[0, i * t:(i + 1) * t, LANES * p:LANES * (p + 1)] = (
                jnp.where(lane < HEAD_DIM, even, out).astype(BF16))


def _attention(qa, ktc, v):
    b, s, _ = qa.shape
    return pl.pallas_call(
        _attn_kernel,
        grid=(b, N_PAIRS // ATT_PAIRS),
        in_specs=[
            pl.BlockSpec((1, s, PAIR_W * ATT_PAIRS), lambda bi, p: (bi, 0, p)),
            pl.BlockSpec((1, 2 * ATT_PAIRS, KT_ROWS, s), lambda bi, p: (bi, p, 0, 0)),
            pl.BlockSpec((1, s, LANES * ATT_PAIRS), lambda bi, p: (bi, 0, p)),
        ],
        out_specs=pl.BlockSpec((1, s, LANES * ATT_PAIRS), lambda bi, p: (bi, 0, p)),
        out_shape=jax.ShapeDtypeStruct((b, s, D_MODEL), BF16),
        scratch_shapes=[
            pltpu.VMEM((2 * ATT_PAIRS, PAIR_W, s), BF16),
            pltpu.VMEM((ATT_PAIRS, s, PAIR_W), BF16),
        ],
        compiler_params=_params("arbitrary", "arbitrary"),
        name="fox_attention",
    )(qa, ktc, v)


def kernel(x, norm_mix, norm_ffn, w_ffn_in, w_ffn_out, w_rec_in, conv_w, conv_b, w_lru_gates,
           b_lru_gates, lru_param, w_rec_out, norm_kv, w_kvf, b_forget, w_q, w_o, norm_final):
    b, s, d = x.shape
    m = b * s
    row = lambda a: a.reshape(1, -1)
    rows = lambda a: a.reshape(a.shape[0], 1, a.shape[1])

    norm_mix, norm_ffn = rows(norm_mix), rows(norm_ffn)
    w_rec_in, w_rec_out = w_rec_in.astype(BF16), w_rec_out.astype(BF16)
    w_lru_gates = (0.5 * w_lru_gates).astype(BF16)
    b_lru_gates = (0.5 * b_lru_gates).reshape(N_A, LRU_BLOCKS, 1, 2 * LRU_BLOCK_W)
    conv_b, lru_param = rows(conv_b), rows(lru_param)

    h = x
    for i in range(N_A):
        last = i == N_A - 1
        rec = _rec_layer(h, i, norm_mix, w_rec_in, conv_w, conv_b, w_lru_gates, b_lru_gates, lru_param, w_rec_out,
                         seq_len=s, batch_major_out=last, casts=((w_ffn_in, 0), (w_ffn_out, 0)) if i == 0 else ())
        if i == 0:
            h, fin0, fout0 = rec
            h, w_ffn_in, w_ffn_out = _ffn(h.reshape(m, d), i, norm_ffn, (fin0, None), (fout0, None),
                                          casts=((w_ffn_in, None), (w_ffn_out, None)))
        elif last:
            h, w_kvf, w_q, w_o = _ffn(rec[0].reshape(m, d), i, norm_ffn, (w_ffn_in, i), (w_ffn_out, i),
                                      casts=((w_kvf, None), (w_q, None), (w_o, None)))
        else:
            h = _ffn(rec[0].reshape(m, d), i, norm_ffn, (w_ffn_in, i), (w_ffn_out, i))
    h = h.reshape(b, s, d)

    w_f = jnp.pad(w_kvf[:, 2 * D_MODEL:], ((0, 0), (0, LANES - N_HEADS)))
    b_f = jnp.pad(b_forget, (0, LANES - N_HEADS)).reshape(1, LANES)
    e = _aug_placement()
    kt, va, c3, qa = _kvf(h, row(norm_kv), w_kvf, w_f, b_f, norm_mix, w_q, e)
    c3 = c3.reshape(m, LANES)

    h2d = h.reshape(m, d)
    for j in range(DEPTH - N_A):
        layer = N_A + j
        o = _attention(qa, kt, va)
        attn = (o.reshape(m, d), w_o, j)
        if layer == DEPTH - 1:
            h2d = _ffn(h2d, layer, norm_ffn, (w_ffn_in, layer), (w_ffn_out, layer), attn=attn,
                       final_g=row(norm_final))
        else:
            h2d, qa = _ffn(h2d, layer, norm_ffn, (w_ffn_in, layer), (w_ffn_out, layer), attn=attn,
                           query=(norm_mix, w_q, j + 1, c3, e))
            qa = qa.reshape(b, s, QA_W)
    return h2d.reshape(b, s, d)
```

```python
import functools
import math

import jax
import jax.numpy as jnp
from jax import lax
from jax.experimental import pallas as pl
from jax.experimental.pallas import tpu as pltpu

D_MODEL = 1024
DEPTH = 4
N_A = DEPTH // 2
D_RNN = D_MODEL
LRU_BLOCK_W = 256
LRU_BLOCKS = D_RNN // LRU_BLOCK_W
CONV_W = 4
LRU_C = 8.0
N_HEADS = 16
HEAD_DIM = D_MODEL // N_HEADS
D_FF = 2816
EPS = 1e-6

SUBLANES = 8
LANES = 128
BF16_ROWS = 2 * SUBLANES
VMEM_LIMIT_BYTES = 56 * 1024 * 1024
TINY = 1e-30

F32 = jnp.float32
BF16 = jnp.bfloat16

ROW_TILE = 512
SEQ_TILE = 512
ATT_TILE = 256
FFN_CHUNKS = ((0, 512), (512, 512), (1024, 512), (1536, 512), (2048, 512), (2560, 256))


def _rmsnorm(x, g):
    y = x * lax.rsqrt(jnp.mean(x * x, axis=-1, keepdims=True) + EPS)
    return y * g


def _sigmoid(x):
    return 1.0 / (1.0 + jnp.exp(-x))


def _softplus(x):
    return jnp.maximum(x, 0.0) + jnp.log1p(jnp.exp(-jnp.abs(x)))


def _sigmoid_of_twice(half_x):
    return 0.5 * jnp.tanh(half_x) + 0.5


def _gelu_tanh(x):
    c = math.sqrt(2.0 / math.pi)
    half = 0.5 * x
    return half + half * jnp.tanh(x * (c + (c * 0.044715) * (x * x)))


def _dot(a, b):
    return jnp.dot(a, b, preferred_element_type=F32)


def _const_spec(shape):
    n = len(shape)
    return pl.BlockSpec(shape, lambda *_: (0,) * n, pipeline_mode=pl.Buffered(1))


def _layer_spec(shape, layer):
    n = len(shape)
    return pl.BlockSpec((None,) + tuple(shape), lambda *_: (layer,) + (0,) * n, pipeline_mode=pl.Buffered(1))


def _weight_spec(shape, layer):
    return _const_spec(shape) if layer is None else _layer_spec(shape, layer)


def _cast_job(w, steps, layer):
    r, c = w.shape[-2:]
    hold = next(k for k in (1, 2, 4, 8) if r % (steps // k) == 0 and (r // (steps // k)) % BF16_ROWS == 0)
    rt = r // (steps // hold)
    if w.ndim == 2:
        spec = pl.BlockSpec((rt, c), lambda i: (i // hold, 0))
        return spec, spec, jax.ShapeDtypeStruct(w.shape, BF16)
    n_layers = w.shape[0]
    if layer is None:
        return (pl.BlockSpec((n_layers, rt, c), lambda i: (0, i // hold, 0)),
                pl.BlockSpec((n_layers, rt, c), lambda i: (0, i // hold, 0)),
                jax.ShapeDtypeStruct(w.shape, BF16))
    return (pl.BlockSpec((None, rt, c), lambda i: (layer, i // hold, 0)),
            pl.BlockSpec((rt, c), lambda i: (i // hold, 0)),
            jax.ShapeDtypeStruct((r, c), BF16))


def _run_casts(srcs, dsts):
    for src, dst in zip(srcs, dsts):
        dst[...] = src[...].astype(BF16)


def _params(*sem):
    return pltpu.CompilerParams(dimension_semantics=sem, vmem_limit_bytes=VMEM_LIMIT_BYTES)


PAIR_W = 2 * LANES
N_PAIRS = N_HEADS // 2
QA_W = N_PAIRS * PAIR_W
C3_ONE = 3 * N_HEADS
KT_ROWS = HEAD_DIM + BF16_ROWS
N_PIECES = 3


def _split3(c):
    hi = c.astype(BF16).astype(F32)
    r = c - hi
    mid = r.astype(BF16).astype(F32)
    lo = (r - mid).astype(BF16).astype(F32)
    return hi, mid, lo


def _aug_placement():
    e = [[0.0] * D_MODEL for _ in range(LANES)]
    for h in range(N_HEADS):
        base = LANES * (h // 2) + HEAD_DIM * (h % 2)
        for piece in range(N_PIECES):
            e[C3_ONE][base + piece] = 1.0
            e[piece * N_HEADS + h][base + N_PIECES + piece] = 1.0
    return jnp.array(e, BF16)


def _query_proj(x, g, wq_ref):
    xn = _rmsnorm(x, g).astype(BF16)
    return _dot(xn, wq_ref[...]) * (HEAD_DIM ** -0.5)


def _store_query(q, c3, e_ref, qa_ref):
    aug = _dot(c3, e_ref[...])
    for p in range(N_PAIRS):
        qa_ref[:, PAIR_W * p:PAIR_W * p + LANES] = q[:, LANES * p:LANES * (p + 1)].astype(BF16)
        qa_ref[:, PAIR_W * p + LANES:PAIR_W * (p + 1)] = aug[:, LANES * p:LANES * (p + 1)].astype(BF16)


def _ffn_kernel(*refs, has_attn, has_final, has_query, n_casts):
    it = iter(refs)
    h_ref = next(it)
    if has_attn:
        o_ref, wo_ref = next(it), next(it)
    g_ref, win_ref, wout_ref = next(it), next(it), next(it)
    if has_final:
        gf_ref = next(it)
    if has_query:
        gq_ref, wq_ref, c3_ref, e_ref = next(it), next(it), next(it), next(it)
    cast_srcs = [next(it) for _ in range(n_casts)]
    out_ref = next(it)
    qa_ref = next(it) if has_query else None
    _run_casts(cast_srcs, [next(it) for _ in range(n_casts)])

    x = h_ref[...]
    if has_attn:
        x = x + _dot(o_ref[...], wo_ref[...])
    xn = _rmsnorm(x, g_ref[...]).astype(BF16)
    acc = x
    for c0, cw in FFN_CHUNKS:
        gate = _dot(xn, win_ref[:, c0:c0 + cw])
        up = _dot(xn, win_ref[:, D_FF + c0:D_FF + c0 + cw])
        act = (gate * _sigmoid(gate)) * up
        acc = acc + _dot(act.astype(BF16), wout_ref[c0:c0 + cw, :])
    if has_final:
        acc = _rmsnorm(acc, gf_ref[...])
    out_ref[...] = acc
    if has_query:
        _store_query(_query_proj(acc, gq_ref[...], wq_ref), c3_ref[...], e_ref, qa_ref)


def _ffn(h2d, layer, norm_ffn, w_in, w_out, attn=None, final_g=None, query=None, casts=()):
    m = h2d.shape[0]
    tile = ROW_TILE
    row = pl.BlockSpec((tile, D_MODEL), lambda i: (i, 0))
    args = [h2d]
    specs = [row]
    if attn is not None:
        o2d, w_o, j = attn
        args += [o2d, w_o]
        specs += [row, _layer_spec((D_MODEL, D_MODEL), j)]
    args += [norm_ffn, w_in[0], w_out[0]]
    specs += [_layer_spec((1, D_MODEL), layer), _weight_spec((D_MODEL, 2 * D_FF), w_in[1]),
              _weight_spec((D_FF, D_MODEL), w_out[1])]
    if final_g is not None:
        args.append(final_g)
        specs.append(_const_spec((1, D_MODEL)))
    out_specs = row
    out_shape = jax.ShapeDtypeStruct((m, D_MODEL), F32)
    if query is not None:
        norm_mix, w_q, j, c3, e = query
        args += [norm_mix, w_q, c3, e]
        specs += [_layer_spec((1, D_MODEL), layer + 1), _layer_spec((D_MODEL, D_MODEL), j),
                  pl.BlockSpec((tile, LANES), lambda i: (i, 0)), _const_spec((LANES, D_MODEL))]
        out_specs = [row, pl.BlockSpec((tile, QA_W), lambda i: (i, 0))]
        out_shape = [out_shape, jax.ShapeDtypeStruct((m, QA_W), BF16)]
    if casts:
        jobs = [_cast_job(w, m // tile, lyr) for w, lyr in casts]
        args += [w for w, _ in casts]
        specs += [j[0] for j in jobs]
        out_specs = ([out_specs] if query is None else out_specs) + [j[1] for j in jobs]
        out_shape = ([out_shape] if query is None else out_shape) + [j[2] for j in jobs]
    return pl.pallas_call(
        functools.partial(_ffn_kernel, has_attn=attn is not None, has_final=final_g is not None,
                          has_query=query is not None, n_casts=len(casts)),
        grid=(m // tile,),
        in_specs=specs,
        out_specs=out_specs,
        out_shape=out_shape,
        compiler_params=_params("parallel"),
        name="ffn",
    )(*args)


REC_STEPS = 64
SLAB_PITCH = REC_STEPS + 4
REC_ROWS = REC_STEPS * SUBLANES
HALO_ROWS = (CONV_W - 1) * SUBLANES


def _rec_kernel(*refs, n_casts, batch_major_in, batch_major_out):
    (h_ref, g_ref, win_ref, cw_ref, cb_ref, wg_ref, bg_ref, lam_ref, wout_ref), refs = refs[:9], refs[9:]
    cast_srcs, out_ref, cast_dsts = refs[:n_casts], refs[n_casts], refs[n_casts + 1:2 * n_casts + 1]
    x_scr, rec_scr, a_scr, u_scr, carry_scr, *slabs = refs[2 * n_casts + 1:]
    rows = REC_ROWS
    _run_casts(cast_srcs, cast_dsts)

    @pl.when(pl.program_id(0) == 0)
    def _():
        rec_scr[0:HALO_ROWS, :] = jnp.zeros((HALO_ROWS, D_RNN), F32)
        carry_scr[...] = jnp.zeros((SUBLANES, D_RNN), F32)

    if batch_major_in:
        for b in range(SUBLANES):
            for c, slab in enumerate(slabs):
                slab[b * SLAB_PITCH:b * SLAB_PITCH + REC_STEPS, :] = h_ref[b, :, c * LANES:(c + 1) * LANES]
        for t in range(REC_STEPS):
            for c, slab in enumerate(slabs):
                x_scr[t * SUBLANES:(t + 1) * SUBLANES, c * LANES:(c + 1) * LANES] = (
                    slab[pl.ds(t, SUBLANES, stride=SLAB_PITCH), :])
        x = x_scr[...]
    else:
        x = h_ref[...]
    xn = _rmsnorm(x, g_ref[...]).astype(BF16)
    proj = _dot(xn, win_ref[...])
    rec_scr[HALO_ROWS:HALO_ROWS + rows, :] = proj[:, D_RNN:]

    conv = cb_ref[...]
    for tap in range(CONV_W):
        conv = conv + rec_scr[tap * SUBLANES:tap * SUBLANES + rows, :] * cw_ref[tap:tap + 1, :]
    rec_scr[0:HALO_ROWS, :] = rec_scr[rows:rows + HALO_ROWS, :]

    for n in range(LRU_BLOCKS):
        lo = n * LRU_BLOCK_W
        xb = conv[:, lo:lo + LRU_BLOCK_W]
        half_gates = _dot(xb.astype(BF16), wg_ref[n]) + bg_ref[n]
        gate_i = _sigmoid_of_twice(half_gates[:, :LRU_BLOCK_W])
        gate_r = _sigmoid_of_twice(half_gates[:, LRU_BLOCK_W:])
        log_a = gate_r * (-LRU_C * _softplus(-lam_ref[:, lo:lo + LRU_BLOCK_W]))
        a = jnp.exp(log_a)
        a_scr[:, lo:lo + LRU_BLOCK_W] = a
        one_m_a2 = 1.0 - a * a
        mult = one_m_a2 * lax.rsqrt(jnp.maximum(one_m_a2, TINY))
        u_scr[:, lo:lo + LRU_BLOCK_W] = xb * gate_i * mult

    hs = carry_scr[...]
    for t in range(REC_STEPS):
        r0 = t * SUBLANES
        hs = a_scr[r0:r0 + SUBLANES, :] * hs + u_scr[r0:r0 + SUBLANES, :]
        u_scr[r0:r0 + SUBLANES, :] = hs
    carry_scr[...] = hs
    y = _gelu_tanh(proj[:, :D_RNN]) * u_scr[...]
    res = x + _dot(y.astype(BF16), wout_ref[...])
    if batch_major_out:
        x_scr[...] = res
        for t in range(REC_STEPS):
            for c, slab in enumerate(slabs):
                slab[pl.ds(t, SUBLANES, stride=SLAB_PITCH), :] = (
                    x_scr[t * SUBLANES:(t + 1) * SUBLANES, c * LANES:(c + 1) * LANES])
        for b in range(SUBLANES):
            for c, slab in enumerate(slabs):
                out_ref[b, :, c * LANES:(c + 1) * LANES] = slab[b * SLAB_PITCH:b * SLAB_PITCH + REC_STEPS, :]
    else:
        out_ref[...] = res


def _rec_layer(h, i, norm_mix, w_in, conv_w, conv_b, w_gates, b_gates, lam, w_out, *, seq_len,
               batch_major_out, casts=()):
    batch_major_in = h.ndim == 3
    b, s = SUBLANES, seq_len
    assert h.size == b * s * D_MODEL, "the (time, batch) row layout needs one batch row per sublane"
    rows = REC_ROWS
    batch_major = pl.BlockSpec((b, REC_STEPS, D_MODEL), lambda si: (0, si, 0))
    time_major = pl.BlockSpec((rows, D_MODEL), lambda si: (si, 0))
    out_shape = (b, s, D_MODEL) if batch_major_out else (s * b, D_MODEL)
    jobs = [_cast_job(w, s // REC_STEPS, lyr) for w, lyr in casts]
    return pl.pallas_call(
        functools.partial(_rec_kernel, n_casts=len(casts), batch_major_in=batch_major_in,
                          batch_major_out=batch_major_out),
        grid=(s // REC_STEPS,),
        in_specs=[
            batch_major if batch_major_in else time_major,
            _layer_spec((1, D_MODEL), i),
            _layer_spec((D_MODEL, 2 * D_RNN), i),
            _layer_spec((CONV_W, D_RNN), i),
            _layer_spec((1, D_RNN), i),
            _layer_spec((LRU_BLOCKS, LRU_BLOCK_W, 2 * LRU_BLOCK_W), i),
            _layer_spec((LRU_BLOCKS, 1, 2 * LRU_BLOCK_W), i),
            _layer_spec((1, D_RNN), i),
            _layer_spec((D_RNN, D_MODEL), i),
        ] + [j[0] for j in jobs],
        out_specs=[batch_major if batch_major_out else time_major] + [j[1] for j in jobs],
        out_shape=[jax.ShapeDtypeStruct(out_shape, F32)] + [j[2] for j in jobs],
        scratch_shapes=[
            pltpu.VMEM((rows, D_MODEL), F32),
            pltpu.VMEM((rows + HALO_ROWS, D_RNN), F32),
            pltpu.VMEM((rows, D_RNN), F32),
            pltpu.VMEM((rows, D_RNN), F32),
            pltpu.VMEM((SUBLANES, D_RNN), F32),
        ] + [pltpu.VMEM((SUBLANES * SLAB_PITCH, LANES), F32)] * (D_MODEL // LANES),
        compiler_params=_params("arbitrary"),
        name="rec_layer",
    )(h, norm_mix, w_in, conv_w, conv_b, w_gates, b_gates, lam, w_out, *[w for w, _ in casts])


def _kvf_kernel(h_ref, g_ref, wkv_ref, wf_ref, bf_ref, gq_ref, wq_ref, e_ref,
                kt_ref, v_ref, c3_ref, qa_ref, carry_scr):
    ts = SEQ_TILE

    @pl.when(pl.program_id(1) == 0)
    def _():
        carry_scr[...] = jnp.zeros((1, LANES), F32)

    x = h_ref[0]
    xn = _rmsnorm(x, g_ref[...]).astype(BF16)
    kv = _dot(xn, wkv_ref[...])

    f_logit = _dot(xn, wf_ref[...]) + bf_ref[...]
    c = -_softplus(-f_logit)
    row = lax.broadcasted_iota(jnp.int32, (ts, LANES), 0)
    k = 1
    while k < ts:
        c = c + jnp.where(row >= k, pltpu.roll(c, k, 0), 0.0)
        k *= 2
    c = c + carry_scr[...]
    carry_scr[...] = c[ts - 1:ts, :]

    lane = lax.broadcasted_iota(jnp.int32, (ts, LANES), 1)
    hi, mid, lo = _split3(c)
    c3 = jnp.where(lane < N_HEADS, hi,
                   jnp.where(lane < 2 * N_HEADS, pltpu.roll(mid, N_HEADS, 1),
                             jnp.where(lane < C3_ONE, pltpu.roll(lo, 2 * N_HEADS, 1),
                                       jnp.where(lane == C3_ONE, 1.0, 0.0)))).astype(BF16)
    c3_ref[0] = c3
    _store_query(_query_proj(x, gq_ref[...], wq_ref), c3, e_ref, qa_ref.at[0])

    v_ref[0] = kv[:, D_MODEL:].astype(BF16)
    c_t = c.T
    arow = lax.broadcasted_iota(jnp.int32, (BF16_ROWS, ts), 0)
    for p in range(N_PAIRS):
        k_t = kv[:, LANES * p:LANES * (p + 1)].T
        for j in range(2):
            h = 2 * p + j
            kt_ref[0, h, 0:HEAD_DIM, :] = k_t[HEAD_DIM * j:HEAD_DIM * (j + 1), :].astype(BF16)
            chi, cmid, clo = _split3(c_t[h:h + 1, :])
            aug = jnp.where(arow == 0, -chi, jnp.where(arow == 1, -cmid, jnp.where(arow == 2, -clo,
                            jnp.where(arow < 2 * N_PIECES, 1.0, 0.0))))
            kt_ref[0, h, HEAD_DIM:KT_ROWS, :] = aug.astype(BF16)


def _kvf(h, norm_kv, w_kv, w_f, b_f, norm_mix, w_q, e):
    b, s, _ = h.shape
    ts = SEQ_TILE
    return pl.pallas_call(
        _kvf_kernel,
        grid=(b, s // ts),
        in_specs=[
            pl.BlockSpec((1, ts, D_MODEL), lambda bi, si: (bi, si, 0)),
            _const_spec((1, D_MODEL)),
            _const_spec((D_MODEL, 2 * D_MODEL)),
            _const_spec((D_MODEL, LANES)),
            _const_spec((1, LANES)),
            _layer_spec((1, D_MODEL), N_A),
            _layer_spec((D_MODEL, D_MODEL), 0),
            _const_spec((LANES, D_MODEL)),
        ],
        out_specs=[
            pl.BlockSpec((1, N_HEADS, KT_ROWS, ts), lambda bi, si: (bi, 0, 0, si)),
            pl.BlockSpec((1, ts, D_MODEL), lambda bi, si: (bi, si, 0)),
            pl.BlockSpec((1, ts, LANES), lambda bi, si: (bi, si, 0)),
            pl.BlockSpec((1, ts, QA_W), lambda bi, si: (bi, si, 0)),
        ],
        out_shape=[
            jax.ShapeDtypeStruct((b, N_HEADS, KT_ROWS, s), BF16),
            jax.ShapeDtypeStruct((b, s, D_MODEL), BF16),
            jax.ShapeDtypeStruct((b, s, LANES), BF16),
            jax.ShapeDtypeStruct((b, s, QA_W), BF16),
        ],
        scratch_shapes=[pltpu.VMEM((1, LANES), F32)],
        compiler_params=_params("parallel", "arbitrary"),
        name="kvf",
    )(h, norm_kv, w_kv, w_f, b_f, norm_mix, w_q, e)


ATT_PAIRS = 4


def _attn_kernel(qa_ref, ktc_ref, v_ref, o_ref, kt_scr, va_scr):
    t = ATT_TILE
    s_len = qa_ref.shape[1]
    tri = (lax.broadcasted_iota(jnp.int32, (t, t), 0) >= lax.broadcasted_iota(jnp.int32, (t, t), 1))
    lane = lax.broadcasted_iota(jnp.int32, (t, LANES), 1)

    @pl.when((pl.program_id(0) == 0) & (pl.program_id(1) == 0))
    def _():
        kt_scr[...] = jnp.zeros(kt_scr.shape, BF16)
        ones_lane = lax.broadcasted_iota(jnp.int32, (s_len, LANES), 1) == 0
        for p in range(ATT_PAIRS):
            va_scr[p, :, LANES:PAIR_W] = jnp.where(ones_lane, 1.0, 0.0).astype(BF16)

    for j in range(2 * ATT_PAIRS):
        own = HEAD_DIM * (j % 2)
        kt_scr[j, own:own + HEAD_DIM, :] = ktc_ref[0, j, 0:HEAD_DIM, :]
        kt_scr[j, LANES + own:LANES + own + BF16_ROWS, :] = ktc_ref[0, j, HEAD_DIM:KT_ROWS, :]
    for p in range(ATT_PAIRS):
        va_scr[p, :, 0:LANES] = v_ref[0, :, LANES * p:LANES * (p + 1)]

    def logits(i, j):
        lo = i * t
        p = j // 2
        s = _dot(qa_ref[0, lo:lo + t, PAIR_W * p:PAIR_W * (p + 1)], kt_scr[j, :, 0:lo + t])
        s_diag = jnp.where(tri, s[:, lo:], -jnp.inf)
        s = jnp.concatenate([s[:, :lo], s_diag], axis=1) if i > 0 else s_diag
        return s, jnp.max(s, axis=-1, keepdims=True)

    def weighted_values(i, j, s, m):
        acc = _dot(jnp.exp(s - m).astype(BF16), va_scr[j // 2, 0:(i + 1) * t, :])
        return acc[:, :LANES] * (1.0 / acc[:, LANES:LANES + 1])

    units = [(i, j) for i in range(s_len // t) for j in range(2 * ATT_PAIRS)]
    pending = logits(*units[0])
    even = None
    for n, (i, j) in enumerate(units):
        ahead = logits(*units[n + 1]) if n + 1 < len(units) else None
        out = weighted_values(i, j, *pending)
        pending = ahead
        if j % 2 == 0:
            even = out
        else:
            p = j // 2
            o_ref[0, i * t:(i + 1) * t, LANES * p:LANES * (p + 1)] = (
                jnp.where(lane < HEAD_DIM, even, out).astype(BF16))


def _attention(qa, ktc, v):
    b, s, _ = qa.shape
    return pl.pallas_call(
        _attn_kernel,
        grid=(b, N_PAIRS // ATT_PAIRS),
        in_specs=[
            pl.BlockSpec((1, s, PAIR_W * ATT_PAIRS), lambda bi, p: (bi, 0, p)),
            pl.BlockSpec((1, 2 * ATT_PAIRS, KT_ROWS, s), lambda bi, p: (bi, p, 0, 0)),
            pl.BlockSpec((1, s, LANES * ATT_PAIRS), lambda bi, p: (bi, 0, p)),
        ],
        out_specs=pl.BlockSpec((1, s, LANES * ATT_PAIRS), lambda bi, p: (bi, 0, p)),
        out_shape=jax.ShapeDtypeStruct((b, s, D_MODEL), BF16),
        scratch_shapes=[
            pltpu.VMEM((2 * ATT_PAIRS, PAIR_W, s), BF16),
            pltpu.VMEM((ATT_PAIRS, s, PAIR_W), BF16),
        ],
        compiler_params=_params("arbitrary", "arbitrary"),
        name="fox_attention",
    )(qa, ktc, v)


def kernel(x, norm_mix, norm_ffn, w_ffn_in, w_ffn_out, w_rec_in, conv_w, conv_b, w_lru_gates,
           b_lru_gates, lru_param, w_rec_out, norm_kv, w_kvf, b_forget, w_q, w_o, norm_final):
    b, s, d = x.shape
    m = b * s
    row = lambda a: a.reshape(1, -1)
    rows = lambda a: a.reshape(a.shape[0], 1, a.shape[1])

    norm_mix, norm_ffn = rows(norm_mix), rows(norm_ffn)
    w_rec_in, w_rec_out = w_rec_in.astype(BF16), w_rec_out.astype(BF16)
    w_lru_gates = (0.5 * w_lru_gates).astype(BF16)
    b_lru_gates = (0.5 * b_lru_gates).reshape(N_A, LRU_BLOCKS, 1, 2 * LRU_BLOCK_W)
    conv_b, lru_param = rows(conv_b), rows(lru_param)

    h = x
    for i in range(N_A):
        last = i == N_A - 1
        rec = _rec_layer(h, i, norm_mix, w_rec_in, conv_w, conv_b, w_lru_gates, b_lru_gates, lru_param, w_rec_out,
                         seq_len=s, batch_major_out=last, casts=((w_ffn_in, 0), (w_ffn_out, 0)) if i == 0 else ())
        if i == 0:
            h, fin0, fout0 = rec
            h, w_ffn_in, w_ffn_out = _ffn(h.reshape(m, d), i, norm_ffn, (fin0, None), (fout0, None),
                                          casts=((w_ffn_in, None), (w_ffn_out, None)))
        elif last:
            h, w_kvf, w_q, w_o = _ffn(rec[0].reshape(m, d), i, norm_ffn, (w_ffn_in, i), (w_ffn_out, i),
                                      casts=((w_kvf, None), (w_q, None), (w_o, None)))
        else:
            h = _ffn(rec[0].reshape(m, d), i, norm_ffn, (w_ffn_in, i), (w_ffn_out, i))
    h = h.reshape(b, s, d)

    w_f = jnp.pad(w_kvf[:, 2 * D_MODEL:], ((0, 0), (0, LANES - N_HEADS)))
    b_f = jnp.pad(b_forget, (0, LANES - N_HEADS)).reshape(1, LANES)
    e = _aug_placement()
    kt, va, c3, qa = _kvf(h, row(norm_kv), w_kvf, w_f, b_f, norm_mix, w_q, e)
    c3 = c3.reshape(m, LANES)

    h2d = h.reshape(m, d)
    for j in range(DEPTH - N_A):
        layer = N_A + j
        o = _attention(qa, kt, va)
        attn = (o.reshape(m, d), w_o, j)
        if layer == DEPTH - 1:
            h2d = _ffn(h2d, layer, norm_ffn, (w_ffn_in, layer), (w_ffn_out, layer), attn=attn,
                       final_g=row(norm_final))
        else:
            h2d, qa = _ffn(h2d, layer, norm_ffn, (w_ffn_in, layer), (w_ffn_out, layer), attn=attn,
                           query=(norm_mix, w_q, j + 1, c3, e))
            qa = qa.reshape(b, s, QA_W)
    return h2d.reshape(b, s, d)
```

```python
import functools
import math

import jax
import jax.numpy as jnp
from jax import lax
from jax.experimental import pallas as pl
from jax.experimental.pallas import tpu as pltpu

D_MODEL = 1024
DEPTH = 4
N_A = DEPTH // 2
D_RNN = D_MODEL
LRU_BLOCK_W = 256
LRU_BLOCKS = D_RNN // LRU_BLOCK_W
CONV_W = 4
LRU_C = 8.0
N_HEADS = 16
HEAD_DIM = D_MODEL // N_HEADS
D_FF = 2816
EPS = 1e-6

SUBLANES = 8
LANES = 128
BF16_ROWS = 2 * SUBLANES
VMEM_LIMIT_BYTES = 56 * 1024 * 1024
TINY = 1e-30

F32 = jnp.float32
BF16 = jnp.bfloat16

ROW_TILE = 512
SEQ_TILE = 512
ATT_TILE = 256
FFN_CHUNKS = ((0, 512), (512, 512), (1024, 512), (1536, 512), (2048, 512), (2560, 256))


def _rmsnorm(x, g):
    y = x * lax.rsqrt(jnp.mean(x * x, axis=-1, keepdims=True) + EPS)
    return y * g


def _sigmoid(x):
    return 1.0 / (1.0 + jnp.exp(-x))


def _softplus(x):
    return jnp.maximum(x, 0.0) + jnp.log1p(jnp.exp(-jnp.abs(x)))


def _sigmoid_of_twice(half_x):
    return 0.5 * jnp.tanh(half_x) + 0.5


def _gelu_tanh(x):
    c = math.sqrt(2.0 / math.pi)
    half = 0.5 * x
    return half + half * jnp.tanh(x * (c + (c * 0.044715) * (x * x)))


def _dot(a, b):
    return jnp.dot(a, b, preferred_element_type=F32)


def _const_spec(shape):
    n = len(shape)
    return pl.BlockSpec(shape, lambda *_: (0,) * n, pipeline_mode=pl.Buffered(1))


def _layer_spec(shape, layer):
    n = len(shape)
    return pl.BlockSpec((None,) + tuple(shape), lambda *_: (layer,) + (0,) * n, pipeline_mode=pl.Buffered(1))


def _weight_spec(shape, layer):
    return _const_spec(shape) if layer is None else _layer_spec(shape, layer)


def _cast_job(w, steps, layer):
    r, c = w.shape[-2:]
    hold = next(k for k in (1, 2, 4, 8) if r % (steps // k) == 0 and (r // (steps // k)) % BF16_ROWS == 0)
    rt = r // (steps // hold)
    if w.ndim == 2:
        spec = pl.BlockSpec((rt, c), lambda i: (i // hold, 0))
        return spec, spec, jax.ShapeDtypeStruct(w.shape, BF16)
    n_layers = w.shape[0]
    if layer is None:
        return (pl.BlockSpec((n_layers, rt, c), lambda i: (0, i // hold, 0)),
                pl.BlockSpec((n_layers, rt, c), lambda i: (0, i // hold, 0)),
                jax.ShapeDtypeStruct(w.shape, BF16))
    return (pl.BlockSpec((None, rt, c), lambda i: (layer, i // hold, 0)),
            pl.BlockSpec((rt, c), lambda i: (i // hold, 0)),
            jax.ShapeDtypeStruct((r, c), BF16))


def _run_casts(srcs, dsts):
    for src, dst in zip(srcs, dsts):
        dst[...] = src[...].astype(BF16)


def _params(*sem):
    return pltpu.CompilerParams(dimension_semantics=sem, vmem_limit_bytes=VMEM_LIMIT_BYTES)


PAIR_W = 2 * LANES
N_PAIRS = N_HEADS // 2
QA_W = N_PAIRS * PAIR_W
C3_ONE = 3 * N_HEADS
KT_ROWS = HEAD_DIM + BF16_ROWS
N_PIECES = 3


def _split3(c):
    hi = c.astype(BF16).astype(F32)
    r = c - hi
    mid = r.astype(BF16).astype(F32)
    lo = (r - mid).astype(BF16).astype(F32)
    return hi, mid, lo


def _aug_placement():
    e = [[0.0] * D_MODEL for _ in range(LANES)]
    for h in range(N_HEADS):
        base = LANES * (h // 2) + HEAD_DIM * (h % 2)
        for piece in range(N_PIECES):
            e[C3_ONE][base + piece] = 1.0
            e[piece * N_HEADS + h][base + N_PIECES + piece] = 1.0
    return jnp.array(e, BF16)


def _query_proj(x, g, wq_ref):
    xn = _rmsnorm(x, g).astype(BF16)
    return _dot(xn, wq_ref[...]) * (HEAD_DIM ** -0.5)


def _store_query(q, c3, e_ref, qa_ref):
    aug = _dot(c3, e_ref[...])
    for p in range(N_PAIRS):
        qa_ref[:, PAIR_W * p:PAIR_W * p + LANES] = q[:, LANES * p:LANES * (p + 1)].astype(BF16)
        qa_ref[:, PAIR_W * p + LANES:PAIR_W * (p + 1)] = aug[:, LANES * p:LANES * (p + 1)].astype(BF16)


def _ffn_kernel(*refs, layer, has_attn, has_final, has_query, n_casts):
    it = iter(refs)
    h_ref = next(it)
    if has_attn:
        o_ref, wo_ref = next(it), next(it)
    g_ref, win_ref, wout_ref = next(it), next(it), next(it)
    if has_final:
        gf_ref = next(it)
    if has_query:
        gq_ref, wq_ref, c3_ref, e_ref = next(it), next(it), next(it), next(it)
    cast_srcs = [next(it) for _ in range(n_casts)]
    out_ref = next(it)
    qa_ref = next(it) if has_query else None
    _run_casts(cast_srcs, [next(it) for _ in range(n_casts)])

    x = h_ref[...]
    if has_attn:
        x = x + _dot(o_ref[...], wo_ref[...])
    xn = _rmsnorm(x, g_ref[layer:layer + 1, :]).astype(BF16)
    acc = x
    for c0, cw in FFN_CHUNKS:
        gate = _dot(xn, win_ref[:, c0:c0 + cw])
        up = _dot(xn, win_ref[:, D_FF + c0:D_FF + c0 + cw])
        act = (gate * _sigmoid(gate)) * up
        acc = acc + _dot(act.astype(BF16), wout_ref[c0:c0 + cw, :])
    if has_final:
        acc = _rmsnorm(acc, gf_ref[...])
    out_ref[...] = acc
    if has_query:
        _store_query(_query_proj(acc, gq_ref[layer + 1:layer + 2, :], wq_ref), c3_ref[...], e_ref, qa_ref)


def _ffn(h2d, layer, norm_ffn, w_in, w_out, attn=None, final_g=None, query=None, casts=()):
    m = h2d.shape[0]
    tile = ROW_TILE
    row = pl.BlockSpec((tile, D_MODEL), lambda i: (i, 0))
    args = [h2d]
    specs = [row]
    if attn is not None:
        o2d, w_o, j = attn
        args += [o2d, w_o]
        specs += [row, _layer_spec((D_MODEL, D_MODEL), j)]
    args += [norm_ffn, w_in[0], w_out[0]]
    specs += [_const_spec((DEPTH, D_MODEL)), _weight_spec((D_MODEL, 2 * D_FF), w_in[1]),
              _weight_spec((D_FF, D_MODEL), w_out[1])]
    if final_g is not None:
        args.append(final_g)
        specs.append(_const_spec((1, D_MODEL)))
    out_specs = row
    out_shape = jax.ShapeDtypeStruct((m, D_MODEL), F32)
    if query is not None:
        norm_mix, w_q, j, c3, e = query
        args += [norm_mix, w_q, c3, e]
        specs += [_const_spec((DEPTH, D_MODEL)), _layer_spec((D_MODEL, D_MODEL), j),
                  pl.BlockSpec((tile, LANES), lambda i: (i, 0)), _const_spec((LANES, D_MODEL))]
        out_specs = [row, pl.BlockSpec((tile, QA_W), lambda i: (i, 0))]
        out_shape = [out_shape, jax.ShapeDtypeStruct((m, QA_W), BF16)]
    if casts:
        jobs = [_cast_job(w, m // tile, lyr) for w, lyr in casts]
        args += [w for w, _ in casts]
        specs += [j[0] for j in jobs]
        out_specs = ([out_specs] if query is None else out_specs) + [j[1] for j in jobs]
        out_shape = ([out_shape] if query is None else out_shape) + [j[2] for j in jobs]
    return pl.pallas_call(
        functools.partial(_ffn_kernel, layer=layer, has_attn=attn is not None, has_final=final_g is not None,
                          has_query=query is not None, n_casts=len(casts)),
        grid=(m // tile,),
        in_specs=specs,
        out_specs=out_specs,
        out_shape=out_shape,
        compiler_params=_params("parallel"),
        name="ffn",
    )(*args)


REC_STEPS = 64
SLAB_PITCH = REC_STEPS + 4
GATE_BRANCH_AFTER_BLOCK = 1
REC_ROWS = REC_STEPS * SUBLANES
HALO_ROWS = (CONV_W - 1) * SUBLANES


def _rec_kernel(*refs, layer, n_casts, batch_major_in, batch_major_out):
    (h_ref, g_ref, win_ref, cw_ref, cb_ref, wg_ref, bg_ref, lam_ref, wout_ref), refs = refs[:9], refs[9:]
    cast_srcs, out_ref, cast_dsts = refs[:n_casts], refs[n_casts], refs[n_casts + 1:2 * n_casts + 1]
    x_scr, rec_scr, a_scr, u_scr, carry_scr, *slabs = refs[2 * n_casts + 1:]
    rows = REC_ROWS
    _run_casts(cast_srcs, cast_dsts)

    @pl.when(pl.program_id(0) == 0)
    def _():
        rec_scr[0:HALO_ROWS, :] = jnp.zeros((HALO_ROWS, D_RNN), F32)
        carry_scr[...] = jnp.zeros((SUBLANES, D_RNN), F32)

    if batch_major_in:
        for b in range(SUBLANES):
            for c, slab in enumerate(slabs):
                slab[b * SLAB_PITCH:b * SLAB_PITCH + REC_STEPS, :] = h_ref[b, :, c * LANES:(c + 1) * LANES]
        for t in range(REC_STEPS):
            for c, slab in enumerate(slabs):
                x_scr[t * SUBLANES:(t + 1) * SUBLANES, c * LANES:(c + 1) * LANES] = (
                    slab[pl.ds(t, SUBLANES, stride=SLAB_PITCH), :])
        x = x_scr[...]
    else:
        x = h_ref[...]
    xn = _rmsnorm(x, g_ref[layer:layer + 1, :]).astype(BF16)
    rec_scr[HALO_ROWS:HALO_ROWS + rows, :] = _dot(xn, win_ref[:, D_RNN:])

    conv = cb_ref[layer:layer + 1, :]
    for tap in range(CONV_W):
        conv = conv + rec_scr[tap * SUBLANES:tap * SUBLANES + rows, :] * cw_ref[tap:tap + 1, :]
    rec_scr[0:HALO_ROWS, :] = rec_scr[rows:rows + HALO_ROWS, :]

    for n in range(LRU_BLOCKS):
        lo = n * LRU_BLOCK_W
        xb = conv[:, lo:lo + LRU_BLOCK_W]
        half_gates = _dot(xb.astype(BF16), wg_ref[n]) + bg_ref[n]
        gate_i = _sigmoid_of_twice(half_gates[:, :LRU_BLOCK_W])
        gate_r = _sigmoid_of_twice(half_gates[:, LRU_BLOCK_W:])
        log_a = gate_r * (-LRU_C * _softplus(-lam_ref[layer:layer + 1, lo:lo + LRU_BLOCK_W]))
        a = jnp.exp(log_a)
        a_scr[:, lo:lo + LRU_BLOCK_W] = a
        one_m_a2 = 1.0 - a * a
        mult = one_m_a2 * lax.rsqrt(jnp.maximum(one_m_a2, TINY))
        u_scr[:, lo:lo + LRU_BLOCK_W] = xb * gate_i * mult
        if n == GATE_BRANCH_AFTER_BLOCK:
            gate_branch = _dot(xn, win_ref[:, :D_RNN])

    hs = carry_scr[...]
    for t in range(REC_STEPS):
        r0 = t * SUBLANES
        hs = a_scr[r0:r0 + SUBLANES, :] * hs + u_scr[r0:r0 + SUBLANES, :]
        u_scr[r0:r0 + SUBLANES, :] = hs
    carry_scr[...] = hs
    y = _gelu_tanh(gate_branch) * u_scr[...]
    res = x + _dot(y.astype(BF16), wout_ref[...])
    if batch_major_out:
        x_scr[...] = res
        for t in range(REC_STEPS):
            for c, slab in enumerate(slabs):
                slab[pl.ds(t, SUBLANES, stride=SLAB_PITCH), :] = (
                    x_scr[t * SUBLANES:(t + 1) * SUBLANES, c * LANES:(c + 1) * LANES])
        for b in range(SUBLANES):
            for c, slab in enumerate(slabs):
                out_ref[b, :, c * LANES:(c + 1) * LANES] = slab[b * SLAB_PITCH:b * SLAB_PITCH + REC_STEPS, :]
    else:
        out_ref[...] = res


def _rec_layer(h, i, norm_mix, w_in, conv_w, conv_b, w_gates, b_gates, lam, w_out, *, seq_len,
               batch_major_out, casts=()):
    batch_major_in = h.ndim == 3
    b, s = SUBLANES, seq_len
    assert h.size == b * s * D_MODEL, "the (time, batch) row layout needs one batch row per sublane"
    rows = REC_ROWS
    batch_major = pl.BlockSpec((b, REC_STEPS, D_MODEL), lambda si: (0, si, 0))
    time_major = pl.BlockSpec((rows, D_MODEL), lambda si: (si, 0))
    out_shape = (b, s, D_MODEL) if batch_major_out else (s * b, D_MODEL)
    jobs = [_cast_job(w, s // REC_STEPS, lyr) for w, lyr in casts]
    return pl.pallas_call(
        functools.partial(_rec_kernel, layer=i, n_casts=len(casts), batch_major_in=batch_major_in,
                          batch_major_out=batch_major_out),
        grid=(s // REC_STEPS,),
        in_specs=[
            batch_major if batch_major_in else time_major,
            _const_spec((DEPTH, D_MODEL)),
            _layer_spec((D_MODEL, 2 * D_RNN), i),
            _layer_spec((CONV_W, D_RNN), i),
            _const_spec((N_A, D_RNN)),
            _layer_spec((LRU_BLOCKS, LRU_BLOCK_W, 2 * LRU_BLOCK_W), i),
            _layer_spec((LRU_BLOCKS, 1, 2 * LRU_BLOCK_W), i),
            _const_spec((N_A, D_RNN)),
            _layer_spec((D_RNN, D_MODEL), i),
        ] + [j[0] for j in jobs],
        out_specs=[batch_major if batch_major_out else time_major] + [j[1] for j in jobs],
        out_shape=[jax.ShapeDtypeStruct(out_shape, F32)] + [j[2] for j in jobs],
        scratch_shapes=[
            pltpu.VMEM((rows, D_MODEL), F32),
            pltpu.VMEM((rows + HALO_ROWS, D_RNN), F32),
            pltpu.VMEM((rows, D_RNN), F32),
            pltpu.VMEM((rows, D_RNN), F32),
            pltpu.VMEM((SUBLANES, D_RNN), F32),
        ] + [pltpu.VMEM((SUBLANES * SLAB_PITCH, LANES), F32)] * (D_MODEL // LANES),
        compiler_params=_params("arbitrary"),
        name="rec_layer",
    )(h, norm_mix, w_in, conv_w, conv_b, w_gates, b_gates, lam, w_out, *[w for w, _ in casts])


def _kvf_kernel(h_ref, g_ref, wkv_ref, wf_ref, bf_ref, gq_ref, wq_ref, e_ref,
                kt_ref, v_ref, c3_ref, qa_ref, carry_scr):
    ts = SEQ_TILE

    @pl.when(pl.program_id(1) == 0)
    def _():
        carry_scr[...] = jnp.zeros((1, LANES), F32)

    x = h_ref[0]
    xn = _rmsnorm(x, g_ref[...]).astype(BF16)
    f_logit = _dot(xn, wf_ref[...]) + bf_ref[...]
    kv = _dot(xn, wkv_ref[...])
    c = -_softplus(-f_logit)
    row = lax.broadcasted_iota(jnp.int32, (ts, LANES), 0)
    k = 1
    while k < ts:
        c = c + jnp.where(row >= k, pltpu.roll(c, k, 0), 0.0)
        k *= 2
    c = c + carry_scr[...]
    carry_scr[...] = c[ts - 1:ts, :]

    lane = lax.broadcasted_iota(jnp.int32, (ts, LANES), 1)
    hi, mid, lo = _split3(c)
    c3 = jnp.where(lane < N_HEADS, hi,
                   jnp.where(lane < 2 * N_HEADS, pltpu.roll(mid, N_HEADS, 1),
                             jnp.where(lane < C3_ONE, pltpu.roll(lo, 2 * N_HEADS, 1),
                                       jnp.where(lane == C3_ONE, 1.0, 0.0)))).astype(BF16)
    c3_ref[0] = c3
    _store_query(_query_proj(x, gq_ref[N_A:N_A + 1, :], wq_ref), c3, e_ref, qa_ref.at[0])

    v_ref[0] = kv[:, D_MODEL:].astype(BF16)
    c_t = c.T
    arow = lax.broadcasted_iota(jnp.int32, (BF16_ROWS, ts), 0)
    for p in range(N_PAIRS):
        k_t = kv[:, LANES * p:LANES * (p + 1)].T
        for j in range(2):
            h = 2 * p + j
            kt_ref[0, h, 0:HEAD_DIM, :] = k_t[HEAD_DIM * j:HEAD_DIM * (j + 1), :].astype(BF16)
            chi, cmid, clo = _split3(c_t[h:h + 1, :])
            aug = jnp.where(arow == 0, -chi, jnp.where(arow == 1, -cmid, jnp.where(arow == 2, -clo,
                            jnp.where(arow < 2 * N_PIECES, 1.0, 0.0))))
            kt_ref[0, h, HEAD_DIM:KT_ROWS, :] = aug.astype(BF16)


def _kvf(h, norm_kv, w_kv, w_f, b_f, norm_mix, w_q, e):
    b, s, _ = h.shape
    ts = SEQ_TILE
    return pl.pallas_call(
        _kvf_kernel,
        grid=(b, s // ts),
        in_specs=[
            pl.BlockSpec((1, ts, D_MODEL), lambda bi, si: (bi, si, 0)),
            _const_spec((1, D_MODEL)),
            _const_spec((D_MODEL, 2 * D_MODEL)),
            _const_spec((D_MODEL, LANES)),
            _const_spec((1, LANES)),
            _const_spec((DEPTH, D_MODEL)),
            _layer_spec((D_MODEL, D_MODEL), 0),
            _const_spec((LANES, D_MODEL)),
        ],
        out_specs=[
            pl.BlockSpec((1, N_HEADS, KT_ROWS, ts), lambda bi, si: (bi, 0, 0, si)),
            pl.BlockSpec((1, ts, D_MODEL), lambda bi, si: (bi, si, 0)),
            pl.BlockSpec((1, ts, LANES), lambda bi, si: (bi, si, 0)),
            pl.BlockSpec((1, ts, QA_W), lambda bi, si: (bi, si, 0)),
        ],
        out_shape=[
            jax.ShapeDtypeStruct((b, N_HEADS, KT_ROWS, s), BF16),
            jax.ShapeDtypeStruct((b, s, D_MODEL), BF16),
            jax.ShapeDtypeStruct((b, s, LANES), BF16),
            jax.ShapeDtypeStruct((b, s, QA_W), BF16),
        ],
        scratch_shapes=[pltpu.VMEM((1, LANES), F32)],
        compiler_params=_params("parallel", "arbitrary"),
        name="kvf",
    )(h, norm_kv, w_kv, w_f, b_f, norm_mix, w_q, e)


ATT_PAIRS = 4


def _attn_kernel(qa_ref, ktc_ref, v_ref, o_ref, kt_scr, va_scr):
    t = ATT_TILE
    s_len = qa_ref.shape[1]
    tri = (lax.broadcasted_iota(jnp.int32, (t, t), 0) >= lax.broadcasted_iota(jnp.int32, (t, t), 1))
    lane = lax.broadcasted_iota(jnp.int32, (t, LANES), 1)

    @pl.when((pl.program_id(0) == 0) & (pl.program_id(1) == 0))
    def _():
        kt_scr[...] = jnp.zeros(kt_scr.shape, BF16)
        ones_lane = lax.broadcasted_iota(jnp.int32, (s_len, LANES), 1) == 0
        for p in range(ATT_PAIRS):
            va_scr[p, :, LANES:PAIR_W] = jnp.where(ones_lane, 1.0, 0.0).astype(BF16)

    for j in range(2 * ATT_PAIRS):
        own = HEAD_DIM * (j % 2)
        kt_scr[j, own:own + HEAD_DIM, :] = ktc_ref[0, j, 0:HEAD_DIM, :]
        kt_scr[j, LANES + own:LANES + own + BF16_ROWS, :] = ktc_ref[0, j, HEAD_DIM:KT_ROWS, :]
    for p in range(ATT_PAIRS):
        va_scr[p, :, 0:LANES] = v_ref[0, :, LANES * p:LANES * (p + 1)]

    def logits(i, j):
        lo = i * t
        p = j // 2
        s = _dot(qa_ref[0, lo:lo + t, PAIR_W * p:PAIR_W * (p + 1)], kt_scr[j, :, 0:lo + t])
        s_diag = jnp.where(tri, s[:, lo:], -jnp.inf)
        s = jnp.concatenate([s[:, :lo], s_diag], axis=1) if i > 0 else s_diag
        return s, jnp.max(s, axis=-1, keepdims=True)

    def weighted_values(i, j, s, m):
        acc = _dot(jnp.exp(s - m).astype(BF16), va_scr[j // 2, 0:(i + 1) * t, :])
        return acc[:, :LANES] * (1.0 / acc[:, LANES:LANES + 1])

    units = [(i, j) for i in range(s_len // t) for j in range(2 * ATT_PAIRS)]
    units = units[2 * ATT_PAIRS:] + units[:2 * ATT_PAIRS]
    pending = logits(*units[0])
    even = None
    for n, (i, j) in enumerate(units):
        ahead = logits(*units[n + 1]) if n + 1 < len(units) else None
        out = weighted_values(i, j, *pending)
        pending = ahead
        if j % 2 == 0:
            even = out
        else:
            p = j // 2
            o_ref[0, i * t:(i + 1) * t, LANES * p:LANES * (p + 1)] = (
                jnp.where(lane < HEAD_DIM, even, out).astype(BF16))


def _attention(qa, ktc, v):
    b, s, _ = qa.shape
    return pl.pallas_call(
        _attn_kernel,
        grid=(b, N_PAIRS // ATT_PAIRS),
        in_specs=[
            pl.BlockSpec((1, s, PAIR_W * ATT_PAIRS), lambda bi, p: (bi, 0, p)),
            pl.BlockSpec((1, 2 * ATT_PAIRS, KT_ROWS, s), lambda bi, p: (bi, p, 0, 0)),
            pl.BlockSpec((1, s, LANES * ATT_PAIRS), lambda bi, p: (bi, 0, p)),
        ],
        out_specs=pl.BlockSpec((1, s, LANES * ATT_PAIRS), lambda bi, p: (bi, 0, p)),
        out_shape=jax.ShapeDtypeStruct((b, s, D_MODEL), BF16),
        scratch_shapes=[
            pltpu.VMEM((2 * ATT_PAIRS, PAIR_W, s), BF16),
            pltpu.VMEM((ATT_PAIRS, s, PAIR_W), BF16),
        ],
        compiler_params=_params("arbitrary", "arbitrary"),
        name="fox_attention",
    )(qa, ktc, v)


def kernel(x, norm_mix, norm_ffn, w_ffn_in, w_ffn_out, w_rec_in, conv_w, conv_b, w_lru_gates,
           b_lru_gates, lru_param, w_rec_out, norm_kv, w_kvf, b_forget, w_q, w_o, norm_final):
    b, s, d = x.shape
    m = b * s
    row = lambda a: a.reshape(1, -1)

    w_rec_in, w_rec_out = w_rec_in.astype(BF16), w_rec_out.astype(BF16)
    w_lru_gates = (0.5 * w_lru_gates).astype(BF16)
    b_lru_gates = (0.5 * b_lru_gates).reshape(N_A, LRU_BLOCKS, 1, 2 * LRU_BLOCK_W)

    h = x
    for i in range(N_A):
        last = i == N_A - 1
        rec = _rec_layer(h, i, norm_mix, w_rec_in, conv_w, conv_b, w_lru_gates, b_lru_gates, lru_param, w_rec_out,
                         seq_len=s, batch_major_out=last, casts=((w_ffn_in, 0), (w_ffn_out, 0)) if i == 0 else ())
        if i == 0:
            h, fin0, fout0 = rec
            h, w_ffn_in, w_ffn_out = _ffn(h.reshape(m, d), i, norm_ffn, (fin0, None), (fout0, None),
                                          casts=((w_ffn_in, None), (w_ffn_out, None)))
        elif last:
            h, w_kvf, w_q, w_o = _ffn(rec[0].reshape(m, d), i, norm_ffn, (w_ffn_in, i), (w_ffn_out, i),
                                      casts=((w_kvf, None), (w_q, None), (w_o, None)))
        else:
            h = _ffn(rec[0].reshape(m, d), i, norm_ffn, (w_ffn_in, i), (w_ffn_out, i))
    h = h.reshape(b, s, d)

    w_f = jnp.pad(w_kvf[:, 2 * D_MODEL:], ((0, 0), (0, LANES - N_HEADS)))
    b_f = jnp.pad(b_forget, (0, LANES - N_HEADS)).reshape(1, LANES)
    e = _aug_placement()
    kt, va, c3, qa = _kvf(h, row(norm_kv), w_kvf, w_f, b_f, norm_mix, w_q, e)
    c3 = c3.reshape(m, LANES)

    h2d = h.reshape(m, d)
    for j in range(DEPTH - N_A):
        layer = N_A + j
        o = _attention(qa, kt, va)
        attn = (o.reshape(m, d), w_o, j)
        if layer == DEPTH - 1:
            h2d = _ffn(h2d, layer, norm_ffn, (w_ffn_in, layer), (w_ffn_out, layer), attn=attn,
                       final_g=row(norm_final))
        else:
            h2d, qa = _ffn(h2d, layer, norm_ffn, (w_ffn_in, layer), (w_ffn_out, layer), attn=attn,
                           query=(norm_mix, w_q, j + 1, c3, e))
            qa = qa.reshape(b, s, QA_W)
    return h2d.reshape(b, s, d)
```

```python
import functools
import math

import jax
import jax.numpy as jnp
from jax import lax
from jax.experimental import pallas as pl
from jax.experimental.pallas import tpu as pltpu

D_MODEL = 1024
DEPTH = 4
N_A = DEPTH // 2
D_RNN = D_MODEL
LRU_BLOCK_W = 256
LRU_BLOCKS = D_RNN // LRU_BLOCK_W
CONV_W = 4
LRU_C = 8.0
N_HEADS = 16
HEAD_DIM = D_MODEL // N_HEADS
D_FF = 2816
EPS = 1e-6

SUBLANES = 8
LANES = 128
BF16_ROWS = 2 * SUBLANES
VMEM_LIMIT_BYTES = 56 * 1024 * 1024
TINY = 1e-30

F32 = jnp.float32
BF16 = jnp.bfloat16

ROW_TILE = 512
SEQ_TILE = 512
ATT_TILE = 256
FFN_CHUNKS = ((0, 512), (512, 512), (1024, 512), (1536, 512), (2048, 512), (2560, 256))


def _rmsnorm(x, g):
    y = x * lax.rsqrt(jnp.mean(x * x, axis=-1, keepdims=True) + EPS)
    return y * g


def _sigmoid(x):
    return 1.0 / (1.0 + jnp.exp(-x))


def _softplus(x):
    return jnp.maximum(x, 0.0) + jnp.log1p(jnp.exp(-jnp.abs(x)))


def _sigmoid_of_twice(half_x):
    return 0.5 * jnp.tanh(half_x) + 0.5


def _gelu_tanh(x):
    c = math.sqrt(2.0 / math.pi)
    half = 0.5 * x
    return half + half * jnp.tanh(x * (c + (c * 0.044715) * (x * x)))


def _dot(a, b):
    return jnp.dot(a, b, preferred_element_type=F32)


def _const_spec(shape):
    n = len(shape)
    return pl.BlockSpec(shape, lambda *_: (0,) * n, pipeline_mode=pl.Buffered(1))


def _layer_spec(shape, layer):
    n = len(shape)
    return pl.BlockSpec((None,) + tuple(shape), lambda *_: (layer,) + (0,) * n, pipeline_mode=pl.Buffered(1))


def _weight_spec(shape, layer):
    return _const_spec(shape) if layer is None else _layer_spec(shape, layer)


def _cast_job(w, steps, layer):
    r, c = w.shape[-2:]
    hold = next(k for k in (1, 2, 4, 8) if r % (steps // k) == 0 and (r // (steps // k)) % BF16_ROWS == 0)
    rt = r // (steps // hold)
    if w.ndim == 2:
        spec = pl.BlockSpec((rt, c), lambda i: (i // hold, 0))
        return spec, spec, jax.ShapeDtypeStruct(w.shape, BF16)
    n_layers = w.shape[0]
    if layer is None:
        return (pl.BlockSpec((n_layers, rt, c), lambda i: (0, i // hold, 0)),
                pl.BlockSpec((n_layers, rt, c), lambda i: (0, i // hold, 0)),
                jax.ShapeDtypeStruct(w.shape, BF16))
    return (pl.BlockSpec((None, rt, c), lambda i: (layer, i // hold, 0)),
            pl.BlockSpec((rt, c), lambda i: (i // hold, 0)),
            jax.ShapeDtypeStruct((r, c), BF16))


def _run_casts(srcs, dsts):
    for src, dst in zip(srcs, dsts):
        dst[...] = src[...].astype(BF16)


def _params(*sem):
    return pltpu.CompilerParams(dimension_semantics=sem, vmem_limit_bytes=VMEM_LIMIT_BYTES)


PAIR_W = 2 * LANES
N_PAIRS = N_HEADS // 2
QA_W = N_PAIRS * PAIR_W
C3_ONE = 3 * N_HEADS
KT_ROWS = HEAD_DIM + BF16_ROWS
N_PIECES = 3


def _split3(c):
    hi = c.astype(BF16).astype(F32)
    r = c - hi
    mid = r.astype(BF16).astype(F32)
    lo = (r - mid).astype(BF16).astype(F32)
    return hi, mid, lo


def _aug_placement():
    e = [[0.0] * D_MODEL for _ in range(LANES)]
    for h in range(N_HEADS):
        base = LANES * (h // 2) + HEAD_DIM * (h % 2)
        for piece in range(N_PIECES):
            e[C3_ONE][base + piece] = 1.0
            e[piece * N_HEADS + h][base + N_PIECES + piece] = 1.0
    return jnp.array(e, BF16)


def _query_proj(x, g, wq_ref):
    xn = _rmsnorm(x, g).astype(BF16)
    return _dot(xn, wq_ref[...]) * (HEAD_DIM ** -0.5)


def _store_query(q, c3, e_ref, qa_ref):
    aug = _dot(c3, e_ref[...])
    for p in range(N_PAIRS):
        qa_ref[:, PAIR_W * p:PAIR_W * p + LANES] = q[:, LANES * p:LANES * (p + 1)].astype(BF16)
        qa_ref[:, PAIR_W * p + LANES:PAIR_W * (p + 1)] = aug[:, LANES * p:LANES * (p + 1)].astype(BF16)


def _ffn_kernel(*refs, layer, has_attn, has_final, has_query, n_casts):
    it = iter(refs)
    h_ref = next(it)
    if has_attn:
        o_ref, wo_ref = next(it), next(it)
    g_ref, win_ref, wout_ref = next(it), next(it), next(it)
    if has_final:
        gf_ref = next(it)
    if has_query:
        gq_ref, wq_ref, c3_ref, e_ref = next(it), next(it), next(it), next(it)
    cast_srcs = [next(it) for _ in range(n_casts)]
    out_ref = next(it)
    qa_ref = next(it) if has_query else None
    _run_casts(cast_srcs, [next(it) for _ in range(n_casts)])

    x = h_ref[...]
    if has_attn:
        x = x + _dot(o_ref[...], wo_ref[...])
    xn = _rmsnorm(x, g_ref[layer:layer + 1, :]).astype(BF16)
    acc = x
    for c0, cw in FFN_CHUNKS:
        gate = _dot(xn, win_ref[:, c0:c0 + cw])
        up = _dot(xn, win_ref[:, D_FF + c0:D_FF + c0 + cw])
        act = (gate * _sigmoid(gate)) * up
        acc = acc + _dot(act.astype(BF16), wout_ref[c0:c0 + cw, :])
    if has_final:
        acc = _rmsnorm(acc, gf_ref[...])
    out_ref[...] = acc
    if has_query:
        _store_query(_query_proj(acc, gq_ref[layer + 1:layer + 2, :], wq_ref), c3_ref[...], e_ref, qa_ref)


def _ffn(h2d, layer, norm_ffn, w_in, w_out, attn=None, final_g=None, query=None, casts=()):
    m = h2d.shape[0]
    tile = ROW_TILE
    row = pl.BlockSpec((tile, D_MODEL), lambda i: (i, 0))
    args = [h2d]
    specs = [row]
    if attn is not None:
        o2d, w_o, j = attn
        args += [o2d, w_o]
        specs += [row, _layer_spec((D_MODEL, D_MODEL), j)]
    args += [norm_ffn, w_in[0], w_out[0]]
    specs += [_const_spec((DEPTH, D_MODEL)), _weight_spec((D_MODEL, 2 * D_FF), w_in[1]),
              _weight_spec((D_FF, D_MODEL), w_out[1])]
    if final_g is not None:
        args.append(final_g)
        specs.append(_const_spec((1, D_MODEL)))
    out_specs = row
    out_shape = jax.ShapeDtypeStruct((m, D_MODEL), F32)
    if query is not None:
        norm_mix, w_q, j, c3, e = query
        args += [norm_mix, w_q, c3, e]
        specs += [_const_spec((DEPTH, D_MODEL)), _layer_spec((D_MODEL, D_MODEL), j),
                  pl.BlockSpec((tile, LANES), lambda i: (i, 0)), _const_spec((LANES, D_MODEL))]
        out_specs = [row, pl.BlockSpec((tile, QA_W), lambda i: (i, 0))]
        out_shape = [out_shape, jax.ShapeDtypeStruct((m, QA_W), BF16)]
    if casts:
        jobs = [_cast_job(w, m // tile, lyr) for w, lyr in casts]
        args += [w for w, _ in casts]
        specs += [j[0] for j in jobs]
        out_specs = ([out_specs] if query is None else out_specs) + [j[1] for j in jobs]
        out_shape = ([out_shape] if query is None else out_shape) + [j[2] for j in jobs]
    return pl.pallas_call(
        functools.partial(_ffn_kernel, layer=layer, has_attn=attn is not None, has_final=final_g is not None,
                          has_query=query is not None, n_casts=len(casts)),
        grid=(m // tile,),
        in_specs=specs,
        out_specs=out_specs,
        out_shape=out_shape,
        compiler_params=_params("parallel"),
        name="ffn",
    )(*args)


REC_STEPS = 64
SLAB_PITCH = REC_STEPS + 4
GATE_BRANCH_AFTER_BLOCK = 1
REC_ROWS = REC_STEPS * SUBLANES
HALO_ROWS = (CONV_W - 1) * SUBLANES


def _rec_kernel(*refs, layer, n_casts, batch_major_in, batch_major_out):
    (h_ref, g_ref, win_ref, cw_ref, cb_ref, wg_ref, bg_ref, lam_ref, wout_ref), refs = refs[:9], refs[9:]
    cast_srcs, out_ref, cast_dsts = refs[:n_casts], refs[n_casts], refs[n_casts + 1:2 * n_casts + 1]
    x_scr, rec_scr, a_scr, u_scr, carry_scr, *slabs = refs[2 * n_casts + 1:]
    rows = REC_ROWS
    _run_casts(cast_srcs, cast_dsts)

    @pl.when(pl.program_id(0) == 0)
    def _():
        rec_scr[0:HALO_ROWS, :] = jnp.zeros((HALO_ROWS, D_RNN), F32)
        carry_scr[...] = jnp.zeros((SUBLANES, D_RNN), F32)

    if batch_major_in:
        for b in range(SUBLANES):
            for c, slab in enumerate(slabs):
                slab[b * SLAB_PITCH:b * SLAB_PITCH + REC_STEPS, :] = h_ref[b, :, c * LANES:(c + 1) * LANES]
        for t in range(REC_STEPS):
            for c, slab in enumerate(slabs):
                x_scr[t * SUBLANES:(t + 1) * SUBLANES, c * LANES:(c + 1) * LANES] = (
                    slab[pl.ds(t, SUBLANES, stride=SLAB_PITCH), :])
        x = x_scr[...]
    else:
        x = h_ref[...]
    xn = _rmsnorm(x, g_ref[layer:layer + 1, :]).astype(BF16)
    rec_scr[HALO_ROWS:HALO_ROWS + rows, :] = _dot(xn, win_ref[:, D_RNN:])

    for n in range(LRU_BLOCKS):
        lo = n * LRU_BLOCK_W
        xb = cb_ref[layer:layer + 1, lo:lo + LRU_BLOCK_W]
        for tap in range(CONV_W):
            xb = xb + (rec_scr[tap * SUBLANES:tap * SUBLANES + rows, lo:lo + LRU_BLOCK_W]
                       * cw_ref[tap:tap + 1, lo:lo + LRU_BLOCK_W])
        half_gates = _dot(xb.astype(BF16), wg_ref[n]) + bg_ref[n]
        gate_i = _sigmoid_of_twice(half_gates[:, :LRU_BLOCK_W])
        gate_r = _sigmoid_of_twice(half_gates[:, LRU_BLOCK_W:])
        log_a = gate_r * (-LRU_C * _softplus(-lam_ref[layer:layer + 1, lo:lo + LRU_BLOCK_W]))
        a = jnp.exp(log_a)
        a_scr[:, lo:lo + LRU_BLOCK_W] = a
        one_m_a2 = 1.0 - a * a
        mult = one_m_a2 * lax.rsqrt(jnp.maximum(one_m_a2, TINY))
        u_scr[:, lo:lo + LRU_BLOCK_W] = xb * gate_i * mult
        if n == GATE_BRANCH_AFTER_BLOCK:
            gate_branch = _dot(xn, win_ref[:, :D_RNN])
    rec_scr[0:HALO_ROWS, :] = rec_scr[rows:rows + HALO_ROWS, :]

    hs = carry_scr[...]
    for t in range(REC_STEPS):
        r0 = t * SUBLANES
        hs = a_scr[r0:r0 + SUBLANES, :] * hs + u_scr[r0:r0 + SUBLANES, :]
        u_scr[r0:r0 + SUBLANES, :] = hs
    carry_scr[...] = hs
    y = _gelu_tanh(gate_branch) * u_scr[...]
    res = x + _dot(y.astype(BF16), wout_ref[...])
    if batch_major_out:
        x_scr[...] = res
        for t in range(REC_STEPS):
            for c, slab in enumerate(slabs):
                slab[pl.ds(t, SUBLANES, stride=SLAB_PITCH), :] = (
                    x_scr[t * SUBLANES:(t + 1) * SUBLANES, c * LANES:(c + 1) * LANES])
        for b in range(SUBLANES):
            for c, slab in enumerate(slabs):
                out_ref[b, :, c * LANES:(c + 1) * LANES] = slab[b * SLAB_PITCH:b * SLAB_PITCH + REC_STEPS, :]
    else:
        out_ref[...] = res


def _rec_layer(h, i, norm_mix, w_in, conv_w, conv_b, w_gates, b_gates, lam, w_out, *, seq_len,
               batch_major_out, casts=()):
    batch_major_in = h.ndim == 3
    b, s = SUBLANES, seq_len
    assert h.size == b * s * D_MODEL, "the (time, batch) row layout needs one batch row per sublane"
    rows = REC_ROWS
    batch_major = pl.BlockSpec((b, REC_STEPS, D_MODEL), lambda si: (0, si, 0))
    time_major = pl.BlockSpec((rows, D_MODEL), lambda si: (si, 0))
    out_shape = (b, s, D_MODEL) if batch_major_out else (s * b, D_MODEL)
    jobs = [_cast_job(w, s // REC_STEPS, lyr) for w, lyr in casts]
    return pl.pallas_call(
        functools.partial(_rec_kernel, layer=i, n_casts=len(casts), batch_major_in=batch_major_in,
                          batch_major_out=batch_major_out),
        grid=(s // REC_STEPS,),
        in_specs=[
            batch_major if batch_major_in else time_major,
            _const_spec((DEPTH, D_MODEL)),
            _layer_spec((D_MODEL, 2 * D_RNN), i),
            _layer_spec((CONV_W, D_RNN), i),
            _const_spec((N_A, D_RNN)),
            _layer_spec((LRU_BLOCKS, LRU_BLOCK_W, 2 * LRU_BLOCK_W), i),
            _layer_spec((LRU_BLOCKS, 1, 2 * LRU_BLOCK_W), i),
            _const_spec((N_A, D_RNN)),
            _layer_spec((D_RNN, D_MODEL), i),
        ] + [j[0] for j in jobs],
        out_specs=[batch_major if batch_major_out else time_major] + [j[1] for j in jobs],
        out_shape=[jax.ShapeDtypeStruct(out_shape, F32)] + [j[2] for j in jobs],
        scratch_shapes=[
            pltpu.VMEM((rows, D_MODEL), F32),
            pltpu.VMEM((rows + HALO_ROWS, D_RNN), F32),
            pltpu.VMEM((rows, D_RNN), F32),
            pltpu.VMEM((rows, D_RNN), F32),
            pltpu.VMEM((SUBLANES, D_RNN), F32),
        ] + [pltpu.VMEM((SUBLANES * SLAB_PITCH, LANES), F32)] * (D_MODEL // LANES),
        compiler_params=_params("arbitrary"),
        name="rec_layer",
    )(h, norm_mix, w_in, conv_w, conv_b, w_gates, b_gates, lam, w_out, *[w for w, _ in casts])


def _kvf_kernel(h_ref, g_ref, wkv_ref, wf_ref, bf_ref, gq_ref, wq_ref, e_ref,
                kt_ref, v_ref, c3_ref, qa_ref, carry_scr):
    ts = SEQ_TILE

    @pl.when(pl.program_id(1) == 0)
    def _():
        carry_scr[...] = jnp.zeros((1, LANES), F32)

    x = h_ref[0]
    xn = _rmsnorm(x, g_ref[...]).astype(BF16)
    f_logit = _dot(xn, wf_ref[...]) + bf_ref[...]
    kv = _dot(xn, wkv_ref[...])
    c = -_softplus(-f_logit)
    row = lax.broadcasted_iota(jnp.int32, (ts, LANES), 0)
    k = 1
    while k < ts:
        c = c + jnp.where(row >= k, pltpu.roll(c, k, 0), 0.0)
        k *= 2
    c = c + carry_scr[...]
    carry_scr[...] = c[ts - 1:ts, :]

    lane = lax.broadcasted_iota(jnp.int32, (ts, LANES), 1)
    hi, mid, lo = _split3(c)
    c3 = jnp.where(lane < N_HEADS, hi,
                   jnp.where(lane < 2 * N_HEADS, pltpu.roll(mid, N_HEADS, 1),
                             jnp.where(lane < C3_ONE, pltpu.roll(lo, 2 * N_HEADS, 1),
                                       jnp.where(lane == C3_ONE, 1.0, 0.0)))).astype(BF16)
    c3_ref[0] = c3
    _store_query(_query_proj(x, gq_ref[N_A:N_A + 1, :], wq_ref), c3, e_ref, qa_ref.at[0])

    v_ref[0] = kv[:, D_MODEL:].astype(BF16)
    c_t = c.T
    arow = lax.broadcasted_iota(jnp.int32, (BF16_ROWS, ts), 0)
    for p in range(N_PAIRS):
        k_t = kv[:, LANES * p:LANES * (p + 1)].T
        for j in range(2):
            h = 2 * p + j
            kt_ref[0, h, 0:HEAD_DIM, :] = k_t[HEAD_DIM * j:HEAD_DIM * (j + 1), :].astype(BF16)
            chi, cmid, clo = _split3(c_t[h:h + 1, :])
            aug = jnp.where(arow == 0, -chi, jnp.where(arow == 1, -cmid, jnp.where(arow == 2, -clo,
                            jnp.where(arow < 2 * N_PIECES, 1.0, 0.0))))
            kt_ref[0, h, HEAD_DIM:KT_ROWS, :] = aug.astype(BF16)


def _kvf(h, norm_kv, w_kv, w_f, b_f, norm_mix, w_q, e):
    b, s, _ = h.shape
    ts = SEQ_TILE
    return pl.pallas_call(
        _kvf_kernel,
        grid=(b, s // ts),
        in_specs=[
            pl.BlockSpec((1, ts, D_MODEL), lambda bi, si: (bi, si, 0)),
            _const_spec((1, D_MODEL)),
            _const_spec((D_MODEL, 2 * D_MODEL)),
            _const_spec((D_MODEL, LANES)),
            _const_spec((1, LANES)),
            _const_spec((DEPTH, D_MODEL)),
            _layer_spec((D_MODEL, D_MODEL), 0),
            _const_spec((LANES, D_MODEL)),
        ],
        out_specs=[
            pl.BlockSpec((1, N_HEADS, KT_ROWS, ts), lambda bi, si: (bi, 0, 0, si)),
            pl.BlockSpec((1, ts, D_MODEL), lambda bi, si: (bi, si, 0)),
            pl.BlockSpec((1, ts, LANES), lambda bi, si: (bi, si, 0)),
            pl.BlockSpec((1, ts, QA_W), lambda bi, si: (bi, si, 0)),
        ],
        out_shape=[
            jax.ShapeDtypeStruct((b, N_HEADS, KT_ROWS, s), BF16),
            jax.ShapeDtypeStruct((b, s, D_MODEL), BF16),
            jax.ShapeDtypeStruct((b, s, LANES), BF16),
            jax.ShapeDtypeStruct((b, s, QA_W), BF16),
        ],
        scratch_shapes=[pltpu.VMEM((1, LANES), F32)],
        compiler_params=_params("parallel", "arbitrary"),
        name="kvf",
    )(h, norm_kv, w_kv, w_f, b_f, norm_mix, w_q, e)


ATT_PAIRS = 4


def _attn_kernel(qa_ref, ktc_ref, v_ref, o_ref, kt_scr, va_scr):
    t = ATT_TILE
    s_len = qa_ref.shape[1]
    tri = (lax.broadcasted_iota(jnp.int32, (t, t), 0) >= lax.broadcasted_iota(jnp.int32, (t, t), 1))
    lane = lax.broadcasted_iota(jnp.int32, (t, LANES), 1)

    @pl.when((pl.program_id(0) == 0) & (pl.program_id(1) == 0))
    def _():
        kt_scr[...] = jnp.zeros(kt_scr.shape, BF16)
        ones_lane = lax.broadcasted_iota(jnp.int32, (s_len, LANES), 1) == 0
        for p in range(ATT_PAIRS):
            va_scr[p, :, LANES:PAIR_W] = jnp.where(ones_lane, 1.0, 0.0).astype(BF16)

    for j in range(2 * ATT_PAIRS):
        own = HEAD_DIM * (j % 2)
        kt_scr[j, own:own + HEAD_DIM, :] = ktc_ref[0, j, 0:HEAD_DIM, :]
        kt_scr[j, LANES + own:LANES + own + BF16_ROWS, :] = ktc_ref[0, j, HEAD_DIM:KT_ROWS, :]
    for p in range(ATT_PAIRS):
        va_scr[p, :, 0:LANES] = v_ref[0, :, LANES * p:LANES * (p + 1)]

    def logits(i, j):
        lo = i * t
        p = j // 2
        s = _dot(qa_ref[0, lo:lo + t, PAIR_W * p:PAIR_W * (p + 1)], kt_scr[j, :, 0:lo + t])
        s_diag = jnp.where(tri, s[:, lo:], -jnp.inf)
        s = jnp.concatenate([s[:, :lo], s_diag], axis=1) if i > 0 else s_diag
        return s, jnp.max(s, axis=-1, keepdims=True)

    def weighted_values(i, j, s, m):
        acc = _dot(jnp.exp(s - m).astype(BF16), va_scr[j // 2, 0:(i + 1) * t, :])
        return acc[:, :LANES] * (1.0 / acc[:, LANES:LANES + 1])

    units = [(i, j) for i in range(s_len // t) for j in range(2 * ATT_PAIRS)]
    pending = logits(*units[0])
    even = None
    for n, (i, j) in enumerate(units):
        ahead = logits(*units[n + 1]) if n + 1 < len(units) else None
        out = weighted_values(i, j, *pending)
        pending = ahead
        if j % 2 == 0:
            even = out
        else:
            p = j // 2
            o_ref[0, i * t:(i + 1) * t, LANES * p:LANES * (p + 1)] = (
                jnp.where(lane < HEAD_DIM, even, out).astype(BF16))


def _attention(qa, ktc, v):
    b, s, _ = qa.shape
    return pl.pallas_call(
        _attn_kernel,
        grid=(b, N_PAIRS // ATT_PAIRS),
        in_specs=[
            pl.BlockSpec((1, s, PAIR_W * ATT_PAIRS), lambda bi, p: (bi, 0, p)),
            pl.BlockSpec((1, 2 * ATT_PAIRS, KT_ROWS, s), lambda bi, p: (bi, p, 0, 0)),
            pl.BlockSpec((1, s, LANES * ATT_PAIRS), lambda bi, p: (bi, 0, p)),
        ],
        out_specs=pl.BlockSpec((1, s, LANES * ATT_PAIRS), lambda bi, p: (bi, 0, p)),
        out_shape=jax.ShapeDtypeStruct((b, s, D_MODEL), BF16),
        scratch_shapes=[
            pltpu.VMEM((2 * ATT_PAIRS, PAIR_W, s), BF16),
            pltpu.VMEM((ATT_PAIRS, s, PAIR_W), BF16),
        ],
        compiler_params=_params("arbitrary", "arbitrary"),
        name="fox_attention",
    )(qa, ktc, v)


def kernel(x, norm_mix, norm_ffn, w_ffn_in, w_ffn_out, w_rec_in, conv_w, conv_b, w_lru_gates,
           b_lru_gates, lru_param, w_rec_out, norm_kv, w_kvf, b_forget, w_q, w_o, norm_final):
    b, s, d = x.shape
    m = b * s
    row = lambda a: a.reshape(1, -1)

    w_rec_in, w_rec_out = w_rec_in.astype(BF16), w_rec_out.astype(BF16)
    w_lru_gates = (0.5 * w_lru_gates).astype(BF16)
    b_lru_gates = (0.5 * b_lru_gates).reshape(N_A, LRU_BLOCKS, 1, 2 * LRU_BLOCK_W)

    h = x
    for i in range(N_A):
        last = i == N_A - 1
        rec = _rec_layer(h, i, norm_mix, w_rec_in, conv_w, conv_b, w_lru_gates, b_lru_gates, lru_param, w_rec_out,
                         seq_len=s, batch_major_out=last, casts=((w_ffn_in, 0), (w_ffn_out, 0)) if i == 0 else ())
        if i == 0:
            h, fin0, fout0 = rec
            h, w_ffn_in, w_ffn_out = _ffn(h.reshape(m, d), i, norm_ffn, (fin0, None), (fout0, None),
                                          casts=((w_ffn_in, None), (w_ffn_out, None)))
        elif last:
            h, w_kvf, w_q, w_o = _ffn(rec[0].reshape(m, d), i, norm_ffn, (w_ffn_in, i), (w_ffn_out, i),
                                      casts=((w_kvf, None), (w_q, None), (w_o, None)))
        else:
            h = _ffn(rec[0].reshape(m, d), i, norm_ffn, (w_ffn_in, i), (w_ffn_out, i))
    h = h.reshape(b, s, d)

    w_f = jnp.pad(w_kvf[:, 2 * D_MODEL:], ((0, 0), (0, LANES - N_HEADS)))
    b_f = jnp.pad(b_forget, (0, LANES - N_HEADS)).reshape(1, LANES)
    e = _aug_placement()
    kt, va, c3, qa = _kvf(h, row(norm_kv), w_kvf, w_f, b_f, norm_mix, w_q, e)
    c3 = c3.reshape(m, LANES)

    h2d = h.reshape(m, d)
    for j in range(DEPTH - N_A):
        layer = N_A + j
        o = _attention(qa, kt, va)
        attn = (o.reshape(m, d), w_o, j)
        if layer == DEPTH - 1:
            h2d = _ffn(h2d, layer, norm_ffn, (w_ffn_in, layer), (w_ffn_out, layer), attn=attn,
                       final_g=row(norm_final))
        else:
            h2d, qa = _ffn(h2d, layer, norm_ffn, (w_ffn_in, layer), (w_ffn_out, layer), attn=attn,
                           query=(norm_mix, w_q, j + 1, c3, e))
            qa = qa.reshape(b, s, QA_W)
    return h2d.reshape(b, s, d)
```
